```python
import jax, jax.numpy as jnp
from jax import lax
import numpy as np

D_MODEL = 1024
BATCH = 8
SEQ = 4096
DEPTH = 1

GRID_W = 64
CTX_LEN = 256
N_HEADS = 4
D_KEY = D_MODEL // 2
D_VAL = D_MODEL
HEAD_K = D_KEY // N_HEADS
HEAD_V = D_VAL // N_HEADS
DECAY_RANK = 16
DECAY_TAU = 16.0
CHUNK = 64
D_CONV = D_MODEL
CONV_W = 3
D_FF = ((-(-8 * D_MODEL // 3)) + 255) // 256 * 256
EPS = 1e-6
IN_SPLITS = (D_KEY, D_KEY, D_VAL, D_VAL, 2 * DECAY_RANK, D_CONV, D_CONV, D_CONV, D_MODEL, D_MODEL)
N_IN = sum(IN_SPLITS)

kernel_name = "hybrid_gla_shortconv_dit_block"


def rmsnorm(x, g):
    xf = x.astype(jnp.float32)
    y = xf * lax.rsqrt(jnp.mean(xf * xf, axis=-1, keepdims=True) + EPS)
    return (y * g.astype(jnp.float32)).astype(x.dtype)


def modulate(h, shift, scale):
    return h * (1.0 + scale[:, None, :]) + shift[:, None, :]


def gla_chunked(q, k, v, log_a, s0):
    Bn, H, T, K = q.shape
    V = v.shape[-1]
    N = T // CHUNK
    f = lambda t: t.astype(jnp.float32).reshape(Bn, H, N, CHUNK, t.shape[-1])
    q, k, v, log_a = f(q), f(k), f(v), f(log_a)
    q = q * (K ** -0.5)
    b = jnp.cumsum(log_a, axis=3)
    b_mid = b[:, :, :, CHUNK // 2 - 1:CHUNK // 2, :]
    b_last = b[:, :, :, CHUNK - 1:, :]
    causal = jnp.tril(jnp.ones((CHUNK, CHUNK), dtype=bool))
    scores = jnp.einsum('bhnik,bhnjk->bhnij', q * jnp.exp(b - b_mid), k * jnp.exp(b_mid - b))
    scores = jnp.where(causal, scores, 0.0)
    o_intra = jnp.einsum('bhnij,bhnjv->bhniv', scores, v)
    kv = jnp.einsum('bhnjk,bhnjv->nbhkv', k * jnp.exp(b_last - b), v)
    decay = jnp.moveaxis(jnp.exp(b_last[:, :, :, 0, :]), 2, 0)

    def step(s, inp):
        d, kv_n = inp
        return d[..., None] * s + kv_n, s

    s_final, s_before = lax.scan(step, s0.astype(jnp.float32), (decay, kv))
    o_inter = jnp.einsum('bhnik,nbhkv->bhniv', q * jnp.exp(b), s_before)
    return (o_intra + o_inter).reshape(Bn, H, T, V), s_final


def gla_bidir(q, k, v, la_f, la_b, s0_f, s0_b):
    o_f, s_f = gla_chunked(q, k, v, la_f, s0_f)
    flip = lambda t: jnp.flip(t, axis=2)
    o_b, s_b = gla_chunked(flip(q), flip(k), flip(v), flip(la_b), s0_b)
    return o_f + flip(o_b), s_f, s_b


def project(h, w_in, w_dec_up, b_dec):
    Bn, T, _ = h.shape
    z = h @ w_in
    q, k, v, g, dd, cb, cc, ch, ga, gb = jnp.split(z, np.cumsum(IN_SPLITS)[:-1].tolist(), axis=-1)
    dd = dd.reshape(Bn, T, 2, DECAY_RANK)
    zd = jnp.einsum('btdr,drk->dbtk', dd, w_dec_up) + b_dec[:, None, None, :]
    log_a = jax.nn.log_sigmoid(zd.astype(jnp.float32)) / DECAY_TAU
    heads = lambda t, hd: t.reshape(Bn, T, N_HEADS, hd).transpose(0, 2, 1, 3)
    return (heads(q, HEAD_K), heads(k, HEAD_K), heads(v, HEAD_V), g,
            heads(log_a[0], HEAD_K), heads(log_a[1], HEAD_K), cb, cc, ch, ga, gb)


def seq_conv(u, w):
    pad = CONV_W // 2
    T = u.shape[1]
    up = jnp.pad(u, ((0, 0), (pad, pad), (0, 0)))
    y = up[:, 0:T] * w[0]
    for j in range(1, CONV_W):
        y = y + up[:, j:j + T] * w[j]
    return y


def merge_branches(o, g, cb, cc, ch, ga, gb, rows, g_head, w_gla_out, w_conv, w_conv_out, w_mix_out):
    Bn, H, T, V = o.shape
    o = rmsnorm(o, g_head).transpose(0, 2, 1, 3).reshape(Bn, T, H * V).astype(g.dtype)
    y_gla = (o * jax.nn.silu(g)) @ w_gla_out
    u = cc * ch
    if rows is None:
        uc = seq_conv(u, w_conv)
    else:
        uc = seq_conv(u.reshape(Bn * rows, T // rows, -1), w_conv).reshape(Bn, T, -1)
    y_conv = (cb * uc) @ w_conv_out
    merged = jax.nn.sigmoid(ga) * y_gla + jax.nn.sigmoid(gb) * y_conv
    return merged @ w_mix_out


def swiglu(h, w_in, w_out):
    a, b = jnp.split(h @ w_in, 2, axis=-1)
    return (jax.nn.silu(a) * b) @ w_out


def setup_inputs(seed: int = 0) -> dict:
    key = jax.random.key(seed)
    ks = jax.random.split(key, 20)
    nrm = lambda k, shape, s: jax.random.normal(k, shape, jnp.float32) * s
    return {
        "x": nrm(ks[0], (BATCH, SEQ, D_MODEL), 1.0),
        "c": nrm(ks[1], (BATCH, D_MODEL), 1.0),
        "ctx": nrm(ks[2], (BATCH, CTX_LEN, D_MODEL), 1.0),
        "c_ctx": nrm(ks[3], (D_MODEL,), 1.0),
        "w_ada": nrm(ks[4], (DEPTH, D_MODEL, 6 * D_MODEL), 0.5 * D_MODEL ** -0.5),
        "b_ada": nrm(ks[5], (DEPTH, 6 * D_MODEL), 0.01),
        "g_mix": 1.0 + nrm(ks[6], (DEPTH, D_MODEL), 0.02),
        "w_in": nrm(ks[7], (DEPTH, D_MODEL, N_IN), D_MODEL ** -0.5),
        "w_dec_up": nrm(ks[8], (DEPTH, 2, DECAY_RANK, D_KEY), DECAY_RANK ** -0.5),
        "b_dec": nrm(ks[9], (DEPTH, 2, D_KEY), 0.1),
        "g_head": 1.0 + nrm(ks[10], (DEPTH, HEAD_V), 0.02),
        "w_gla_out": nrm(ks[11], (DEPTH, D_VAL, D_MODEL), D_VAL ** -0.5),
        "w_conv": nrm(ks[12], (DEPTH, CONV_W, D_CONV), CONV_W ** -0.5),
        "w_conv_out": nrm(ks[13], (DEPTH, D_CONV, D_MODEL), D_CONV ** -0.5),
        "w_mix_out": nrm(ks[14], (DEPTH, D_MODEL, D_MODEL), D_MODEL ** -0.5),
        "g_ffn": 1.0 + nrm(ks[15], (DEPTH, D_MODEL), 0.02),
        "w_ffn_in": nrm(ks[16], (DEPTH, D_MODEL, 2 * D_FF), D_MODEL ** -0.5),
        "w_ffn_out": nrm(ks[17], (DEPTH, D_FF, D_MODEL), D_FF ** -0.5),
        "g_final": 1.0 + nrm(ks[18], (D_MODEL,), 0.02),
    }


def reference(x, c, ctx, c_ctx, w_ada, b_ada, g_mix, w_in, w_dec_up, b_dec, g_head, w_gla_out,
              w_conv, w_conv_out, w_mix_out, g_ffn, w_ffn_in, w_ffn_out, g_final):
    rows = x.shape[1] // GRID_W
    bc = ctx.shape[0]
    zeros_state = jnp.zeros((bc, N_HEADS, HEAD_K, HEAD_V), jnp.float32)
    for l in range(DEPTH):
        last = l == DEPTH - 1
        mod_lat = jnp.split(jax.nn.silu(c) @ w_ada[l] + b_ada[l], 6, axis=-1)
        mod_ctx = jnp.split(jax.nn.silu(c_ctx)[None, :] @ w_ada[l] + b_ada[l], 6, axis=-1)
        sh1, sc1, gt1, sh2, sc2, gt2 = mod_lat
        sh1c, sc1c, gt1c, sh2c, sc2c, gt2c = mod_ctx

        pl = project(modulate(rmsnorm(x, g_mix[l]), sh1, sc1), w_in[l], w_dec_up[l], b_dec[l])
        pc = project(modulate(rmsnorm(ctx, g_mix[l]), sh1c, sc1c), w_in[l], w_dec_up[l], b_dec[l])
        o_c, s_cf, s_cb = gla_bidir(pc[0], pc[1], pc[2], pc[4], pc[5], zeros_state, zeros_state)
        o_l, _, _ = gla_bidir(pl[0], pl[1], pl[2], pl[4], pl[5], s_cf, s_cb)
        mix_l = merge_branches(o_l, pl[3], pl[6], pl[7], pl[8], pl[9], pl[10], rows, g_head[l],
                               w_gla_out[l], w_conv[l], w_conv_out[l], w_mix_out[l])
        x = x + gt1[:, None, :] * mix_l
        if not last:
            mix_c = merge_branches(o_c, pc[3], pc[6], pc[7], pc[8], pc[9], pc[10], None, g_head[l],
                                   w_gla_out[l], w_conv[l], w_conv_out[l], w_mix_out[l])
            ctx = ctx + gt1c[:, None, :] * mix_c

        x = x + gt2[:, None, :] * swiglu(modulate(rmsnorm(x, g_ffn[l]), sh2, sc2), w_ffn_in[l], w_ffn_out[l])
        if not last:
            ctx = ctx + gt2c[:, None, :] * swiglu(modulate(rmsnorm(ctx, g_ffn[l]), sh2c, sc2c),
                                                  w_ffn_in[l], w_ffn_out[l])
    return rmsnorm(x, g_final)
```

```python
import functools

import jax
import jax.numpy as jnp
from jax import lax
from jax.experimental import pallas as pl
from jax.experimental.pallas import tpu as pltpu

D_MODEL = 1024
N_HEADS = 4
D_KEY = 512
D_VAL = 1024
HEAD_K = D_KEY // N_HEADS
HEAD_V = D_VAL // N_HEADS
DECAY_RANK = 16
DECAY_TAU = 16.0
CHUNK = 64
GRID_W = 64
D_FF = 2816
EPS = 1e-6

TM_PROJ = 256
TM_FFN = 512
FF_SPLIT = 2
VMEM_LIMIT = 60 * 1024 * 1024

BF16 = jnp.bfloat16
F32 = jnp.float32

NT_DIMS = (((1,), (1,)), ((), ()))
TN_DIMS = (((0,), (0,)), ((), ()))


def _dot(a, b):
    return jnp.dot(a, b, preferred_element_type=F32)


def _sigmoid(x):
    return 1.0 / (1.0 + jnp.exp(-x))


def _silu(x):
    return x * _sigmoid(x)


def _log_sigmoid(x):
    return jnp.minimum(x, 0.0) - jnp.log(1.0 + jnp.exp(-jnp.abs(x)))


def _rms(x):
    return x * lax.rsqrt(jnp.mean(x * x, axis=-1, keepdims=True) + EPS)


def _cumsum_matrix(n, upper):
    r = lax.broadcasted_iota(jnp.int32, (n, n), 0)
    c = lax.broadcasted_iota(jnp.int32, (n, n), 1)
    same = (r // CHUNK) == (c // CHUNK)
    tri = (c >= r) if upper else (c <= r)
    return jnp.where(same & tri, 1.0, 0.0).astype(BF16)


def _chunk_cumsum(mat, la):
    hi = la.astype(BF16)
    lo = (la - hi.astype(F32)).astype(BF16)
    return _dot(mat, hi) + _dot(mat, lo)


def _log_decay(hb, wdd_ref, wdec_ref, bdec_ref):
    dd = _dot(hb, wdd_ref[...]).astype(BF16)
    zd = _dot(dd, wdec_ref[...]) + bdec_ref[...]
    return _log_sigmoid(zd) * (1.0 / DECAY_TAU)


def _gla_chunk(qc, kc, vc, st_ref, e_mid, e_last, e_gap, upper):
    r = lax.broadcasted_iota(jnp.int32, (CHUNK, CHUNK), 0)
    c = lax.broadcasted_iota(jnp.int32, (CHUNK, CHUNK), 1)
    keep = (c >= r) if upper else (c <= r)
    outs = []
    for h in range(N_HEADS):
        ks = slice(h * HEAD_K, (h + 1) * HEAD_K)
        vs = slice(h * HEAD_V, (h + 1) * HEAD_V)
        q, k, v = qc[:, ks], kc[:, ks], vc[:, vs]
        st = st_ref[:, ks]
        scores = lax.dot_general(q, k, NT_DIMS, preferred_element_type=F32)
        a = jnp.where(keep, scores, 0.0).astype(BF16)
        st_in = (st * e_mid[:, ks]).astype(BF16)
        o = _dot(a, v) + lax.dot_general(q, st_in, NT_DIMS, preferred_element_type=F32)
        kv_t = lax.dot_general(v, k, TN_DIMS, preferred_element_type=F32)
        st_ref[:, ks] = st * e_last[:, ks] + kv_t * e_gap[:, ks]
        outs.append(o)
    return jnp.concatenate(outs, axis=-1)


def _state_update_only(kc, vc, st_ref, e_last, e_gap):
    for h in range(N_HEADS):
        ks = slice(h * HEAD_K, (h + 1) * HEAD_K)
        vs = slice(h * HEAD_V, (h + 1) * HEAD_V)
        kv_t = lax.dot_general(vc[:, vs], kc[:, ks], TN_DIMS, preferred_element_type=F32)
        st_ref[:, ks] = st_ref[:, ks] * e_last[:, ks] + kv_t * e_gap[:, ks]


def _ada_kernel(c_ref, w_ref, b_ref, o_ref):
    s = _silu(c_ref[...]).astype(BF16)
    o_ref[...] = _dot(s, w_ref[...].astype(BF16)) + b_ref[...]


def _ada_call(c_rows, w_ada, b_ada):
    n_rows = c_rows.shape[0]
    n_out = w_ada.shape[1]
    bn = 512
    return pl.pallas_call(
        _ada_kernel,
        grid=(n_out // bn,),
        in_specs=[
            pl.BlockSpec((n_rows, D_MODEL), lambda j: (0, 0)),
            pl.BlockSpec((D_MODEL, bn), lambda j: (0, j)),
            pl.BlockSpec((1, bn), lambda j: (0, j)),
        ],
        out_specs=pl.BlockSpec((n_rows, bn), lambda j: (0, j)),
        out_shape=jax.ShapeDtypeStruct((n_rows, n_out), F32),
        name="ada",
    )(c_rows, w_ada, b_ada)


def _ctx_kernel(x_ref, mod_ref, g_ref, wk_ref, wv_ref, wdd_ref, wdec_ref, bdec_ref,
                sf_ref, sb_ref):
    n = x_ref.shape[1]
    nch = n // CHUNK
    x = x_ref[0]
    h = _rms(x) * g_ref[...] * (1.0 + mod_ref[0, 1:2, :]) + mod_ref[0, 0:1, :]
    hb = h.astype(BF16)
    la = _log_decay(hb, wdd_ref, wdec_ref, bdec_ref)
    b_f = _chunk_cumsum(_cumsum_matrix(n, False), la[:, :D_KEY])
    b_b = _chunk_cumsum(_cumsum_matrix(n, True), la[:, D_KEY:])
    k = _dot(hb, wk_ref[...])
    v = _dot(hb, wv_ref[...]).astype(BF16)
    sf_ref[0] = jnp.zeros(sf_ref.shape[1:], F32)
    sb_ref[0] = jnp.zeros(sb_ref.shape[1:], F32)
    mid = CHUNK // 2
    for ci in range(nch):
        rows = slice(ci * CHUNK, (ci + 1) * CHUNK)
        bc = b_f[rows]
        b_mid, b_last = bc[mid - 1:mid], bc[CHUNK - 1:CHUNK]
        kc = (k[rows] * jnp.exp(b_mid - bc)).astype(BF16)
        _state_update_only(kc, v[rows], sf_ref.at[0], jnp.exp(b_last), jnp.exp(b_last - b_mid))
    for ci in reversed(range(nch)):
        rows = slice(ci * CHUNK, (ci + 1) * CHUNK)
        bc = b_b[rows]
        b_mid, b_last = bc[mid:mid + 1], bc[0:1]
        kc = (k[rows] * jnp.exp(b_mid - bc)).astype(BF16)
        _state_update_only(kc, v[rows], sb_ref.at[0], jnp.exp(b_last), jnp.exp(b_last - b_mid))


def _ctx_call(ctx, mod, g_mix, w_k, w_v, w_dd, w_dec, b_dec, ctx_row):
    bsz, n, _ = ctx.shape
    const = lambda b: (0, 0)
    st_spec = pl.BlockSpec((1, HEAD_V, D_KEY), lambda b: (b, 0, 0))
    st_shape = jax.ShapeDtypeStruct((bsz, HEAD_V, D_KEY), F32)
    return pl.pallas_call(
        _ctx_kernel,
        grid=(bsz,),
        in_specs=[
            pl.BlockSpec((1, n, D_MODEL), lambda b: (b, 0, 0)),
            pl.BlockSpec((1, 6, D_MODEL), lambda b: (ctx_row, 0, 0)),
            pl.BlockSpec((1, D_MODEL), const),
            pl.BlockSpec(w_k.shape, const),
            pl.BlockSpec(w_v.shape, const),
            pl.BlockSpec(w_dd.shape, const),
            pl.BlockSpec(w_dec.shape, const),
            pl.BlockSpec(b_dec.shape, const),
        ],
        out_specs=[st_spec, st_spec],
        out_shape=[st_shape, st_shape],
        compiler_params=pltpu.CompilerParams(vmem_limit_bytes=VMEM_LIMIT),
        name="ctx_states",
    )(ctx, mod, g_mix, w_k, w_v, w_dd, w_dec, b_dec)


def _proj_bwd_kernel(x_ref, mod_ref, g_ref, wqkvg_ref, wdd_ref, wdec_ref, bdec_ref, wc3_ref,
                     wgate_ref, wconv_ref, wco_ref, s0_ref,
                     qf_ref, kf_ref, v_ref, ob_ref, gs_ref, sga_ref, mb_ref, fst_ref,
                     st_ref):
    tm = x_ref.shape[1]
    nch = tm // CHUNK
    mid = CHUNK // 2

    @pl.when(pl.program_id(1) == 0)
    def _():
        st_ref[...] = s0_ref[0]

    x = x_ref[0]
    h = _rms(x) * g_ref[...] * (1.0 + mod_ref[0, 1:2, :]) + mod_ref[0, 0:1, :]
    hb = h.astype(BF16)

    la = _log_decay(hb, wdd_ref, wdec_ref, bdec_ref)
    b_f = _chunk_cumsum(_cumsum_matrix(tm, False), la[:, :D_KEY])
    b_b = _chunk_cumsum(_cumsum_matrix(tm, True), la[:, D_KEY:])

    q = _dot(hb, wqkvg_ref[:, 0:D_KEY]) * (HEAD_K ** -0.5)
    k = _dot(hb, wqkvg_ref[:, D_KEY:2 * D_KEY])
    vb = _dot(hb, wqkvg_ref[:, 2 * D_KEY:2 * D_KEY + D_VAL]).astype(BF16)
    v_ref[0] = vb

    for ci in reversed(range(nch)):
        rows = slice(ci * CHUNK, (ci + 1) * CHUNK)
        bc = b_f[rows]
        b_mid, b_last = bc[mid - 1:mid], bc[CHUNK - 1:CHUNK]
        d = bc - b_mid
        qf_ref[0, rows, :] = (q[rows] * jnp.exp(d)).astype(BF16)
        kf_ref[0, rows, :] = (k[rows] * jnp.exp(-d)).astype(BF16)
        fst_ref[0, 0, ci:ci + 1, :] = jnp.concatenate([b_mid, b_last], axis=-1)
        bc = b_b[rows]
        b_mid, b_last = bc[mid:mid + 1], bc[0:1]
        d = bc - b_mid
        qc = (q[rows] * jnp.exp(d)).astype(BF16)
        kc = (k[rows] * jnp.exp(-d)).astype(BF16)
        ob_ref[0, rows, :] = _gla_chunk(qc, kc, vb[rows], st_ref, jnp.exp(b_mid),
                                        jnp.exp(b_last), jnp.exp(b_last - b_mid), True)

    g = _dot(hb, wqkvg_ref[:, 2 * D_KEY + D_VAL:])
    gs_ref[0] = _silu(g).astype(BF16)

    cb = _dot(hb, wc3_ref[:, 0:D_MODEL])
    u = _dot(hb, wc3_ref[:, D_MODEL:2 * D_MODEL]) * _dot(hb, wc3_ref[:, 2 * D_MODEL:])
    pos = lax.broadcasted_iota(jnp.int32, (tm, 1), 0) % GRID_W
    u_prev = jnp.where(pos == 0, 0.0, pltpu.roll(u, 1, 0))
    u_next = jnp.where(pos == GRID_W - 1, 0.0, pltpu.roll(u, tm - 1, 0))
    uc = u_prev * wconv_ref[0:1, :] + u * wconv_ref[1:2, :] + u_next * wconv_ref[2:3, :]
    y_conv = _dot((cb * uc).astype(BF16), wco_ref[...])
    gb = _dot(hb, wgate_ref[:, D_MODEL:])
    mb_ref[0] = (_sigmoid(gb) * y_conv).astype(BF16)
    ga = _dot(hb, wgate_ref[:, 0:D_MODEL])
    sga_ref[0] = _sigmoid(ga).astype(BF16)


def _proj_bwd_call(x, mod, g_mix, w_qkvg, w_dd, w_dec, b_dec, w_c3, w_gate, w_conv, w_co, s0):
    bsz, t, _ = x.shape
    tm = TM_PROJ
    nt = t // tm
    nch = tm // CHUNK
    rev = lambda b, i: (b, nt - 1 - i, 0)
    const = lambda b, i: (0, 0)
    tok = lambda width: pl.BlockSpec((1, tm, width), rev)
    tok_shape = lambda width, dt: jax.ShapeDtypeStruct((bsz, t, width), dt)
    return pl.pallas_call(
        _proj_bwd_kernel,
        grid=(bsz, nt),
        in_specs=[
            tok(D_MODEL),
            pl.BlockSpec((1, 6, D_MODEL), lambda b, i: (b, 0, 0)),
            pl.BlockSpec((1, D_MODEL), const),
            pl.BlockSpec(w_qkvg.shape, const),
            pl.BlockSpec(w_dd.shape, const),
            pl.BlockSpec(w_dec.shape, const),
            pl.BlockSpec(b_dec.shape, const),
            pl.BlockSpec(w_c3.shape, const),
            pl.BlockSpec(w_gate.shape, const),
            pl.BlockSpec(w_conv.shape, const),
            pl.BlockSpec(w_co.shape, const),
            pl.BlockSpec((1, HEAD_V, D_KEY), lambda b, i: (b, 0, 0)),
        ],
        out_specs=[
            tok(D_KEY), tok(D_KEY), tok(D_VAL), tok(D_VAL), tok(D_VAL), tok(D_MODEL), tok(D_MODEL),
            pl.BlockSpec((1, 1, nch, 2 * D_KEY), lambda b, i: (b, nt - 1 - i, 0, 0)),
        ],
        out_shape=[
            tok_shape(D_KEY, BF16), tok_shape(D_KEY, BF16), tok_shape(D_VAL, BF16),
            tok_shape(D_VAL, F32), tok_shape(D_VAL, BF16), tok_shape(D_MODEL, BF16),
            tok_shape(D_MODEL, BF16),
            jax.ShapeDtypeStruct((bsz, nt, nch, 2 * D_KEY), F32),
        ],
        scratch_shapes=[pltpu.VMEM((HEAD_V, D_KEY), F32)],
        compiler_params=pltpu.CompilerParams(
            dimension_semantics=("arbitrary", "arbitrary"), vmem_limit_bytes=VMEM_LIMIT),
        name="proj_bwd",
    )(x, mod, g_mix, w_qkvg, w_dd, w_dec, b_dec, w_c3, w_gate, w_conv, w_co, s0)


def _mix_fwd_kernel(x_ref, mod_ref, qf_ref, kf_ref, v_ref, ob_ref, gs_ref, sga_ref, mb_ref,
                    fst_ref, gh_ref, wgo_ref, wmo_ref, s0_ref, o_ref, st_ref, on_ref):
    tm = x_ref.shape[1]
    nch = tm // CHUNK

    @pl.when(pl.program_id(1) == 0)
    def _():
        st_ref[...] = s0_ref[0]

    for ci in range(nch):
        rows = slice(ci * CHUNK, (ci + 1) * CHUNK)
        b_mid = fst_ref[0, 0, ci:ci + 1, 0:D_KEY]
        b_last = fst_ref[0, 0, ci:ci + 1, D_KEY:]
        o = _gla_chunk(qf_ref[0, rows, :], kf_ref[0, rows, :], v_ref[0, rows, :], st_ref,
                       jnp.exp(b_mid), jnp.exp(b_last), jnp.exp(b_last - b_mid), False)
        o = o + ob_ref[0, rows, :]
        for h in range(N_HEADS):
            vs = slice(h * HEAD_V, (h + 1) * HEAD_V)
            on_ref[rows, vs] = _rms(o[:, vs]) * gh_ref[...]

    a = (on_ref[...] * gs_ref[0].astype(F32)).astype(BF16)
    y_gla = _dot(a, wgo_ref[...])
    merged = sga_ref[0].astype(F32) * y_gla + mb_ref[0].astype(F32)
    mix = _dot(merged.astype(BF16), wmo_ref[...])
    o_ref[0] = x_ref[0] + mod_ref[0, 2:3, :] * mix


def _mix_fwd_call(x, mod, qf, kf, v, ob, gs, sga, mb, fst, g_head, w_go, w_mo, s0):
    bsz, t, _ = x.shape
    tm = TM_PROJ
    nt = t // tm
    nch = tm // CHUNK
    fwd = lambda b, i: (b, i, 0)
    const = lambda b, i: (0, 0)
    tok = lambda width: pl.BlockSpec((1, tm, width), fwd)
    return pl.pallas_call(
        _mix_fwd_kernel,
        grid=(bsz, nt),
        in_specs=[
            tok(D_MODEL),
            pl.BlockSpec((1, 6, D_MODEL), lambda b, i: (b, 0, 0)),
            tok(D_KEY), tok(D_KEY), tok(D_VAL), tok(D_VAL), tok(D_VAL), tok(D_MODEL), tok(D_MODEL),
            pl.BlockSpec((1, 1, nch, 2 * D_KEY), lambda b, i: (b, i, 0, 0)),
            pl.BlockSpec((1, HEAD_V), const),
            pl.BlockSpec(w_go.shape, const),
            pl.BlockSpec(w_mo.shape, const),
            pl.BlockSpec((1, HEAD_V, D_KEY), lambda b, i: (b, 0, 0)),
        ],
        out_specs=tok(D_MODEL),
        out_shape=jax.ShapeDtypeStruct((bsz, t, D_MODEL), F32),
        scratch_shapes=[pltpu.VMEM((HEAD_V, D_KEY), F32), pltpu.VMEM((tm, D_VAL), F32)],
        compiler_params=pltpu.CompilerParams(
            dimension_semantics=("arbitrary", "arbitrary"), vmem_limit_bytes=VMEM_LIMIT),
        name="mix_fwd",
    )(x, mod, qf, kf, v, ob, gs, sga, mb, fst, g_head, w_go, w_mo, s0)


def _ffn_kernel(x_ref, mod_ref, g_ref, win_ref, wout_ref, gfin_ref, o_ref):
    x = x_ref[0]
    h = _rms(x) * g_ref[...] * (1.0 + mod_ref[0, 4:5, :]) + mod_ref[0, 3:4, :]
    hb = h.astype(BF16)
    width = D_FF // FF_SPLIT
    y = None
    for j in range(FF_SPLIT):
        cols = slice(j * width, (j + 1) * width)
        gate_cols = slice(D_FF + j * width, D_FF + (j + 1) * width)
        t = (_silu(_dot(hb, win_ref[:, cols])) * _dot(hb, win_ref[:, gate_cols])).astype(BF16)
        part = _dot(t, wout_ref[cols, :])
        y = part if y is None else y + part
    x2 = x + mod_ref[0, 5:6, :] * y
    o_ref[0] = _rms(x2) * gfin_ref[...]


def _ffn_call(x, mod, g_ffn, w_in, w_out, g_final):
    bsz, t, _ = x.shape
    tm = TM_FFN
    const = lambda b, i: (0, 0)
    tok = pl.BlockSpec((1, tm, D_MODEL), lambda b, i: (b, i, 0))
    return pl.pallas_call(
        _ffn_kernel,
        grid=(bsz, t // tm),
        in_specs=[
            tok,
            pl.BlockSpec((1, 6, D_MODEL), lambda b, i: (b, 0, 0)),
            pl.BlockSpec((1, D_MODEL), const),
            pl.BlockSpec(w_in.shape, const),
            pl.BlockSpec(w_out.shape, const),
            pl.BlockSpec((1, D_MODEL), const),
        ],
        out_specs=tok,
        out_shape=jax.ShapeDtypeStruct((bsz, t, D_MODEL), F32),
        compiler_params=pltpu.CompilerParams(
            dimension_semantics=("arbitrary", "arbitrary"), vmem_limit_bytes=VMEM_LIMIT),
        name="ffn",
    )(x, mod, g_ffn, w_in, w_out, g_final)


def kernel(x, c, ctx, c_ctx, w_ada, b_ada, g_mix, w_in, w_dec_up, b_dec, g_head, w_gla_out,
           w_conv, w_conv_out, w_mix_out, g_ffn, w_ffn_in, w_ffn_out, g_final):
    assert w_ada.shape[0] == 1, "single-layer block"
    bsz = x.shape[0]
    assert x.shape[1] % TM_PROJ == 0 and x.shape[1] % TM_FFN == 0 and ctx.shape[1] % CHUNK == 0

    wi = w_in[0].astype(BF16)
    o_g = 2 * D_KEY + 2 * D_VAL
    o_c = o_g + 2 * DECAY_RANK
    o_gate = o_c + 3 * D_MODEL
    w_qkvg, w_dd, w_c3, w_gate = wi[:, :o_g], wi[:, o_g:o_c], wi[:, o_c:o_gate], wi[:, o_gate:]
    zeros = jnp.zeros((DECAY_RANK, D_KEY), F32)
    w_dec = jnp.concatenate([jnp.concatenate([w_dec_up[0, 0], zeros], axis=1),
                             jnp.concatenate([zeros, w_dec_up[0, 1]], axis=1)], axis=0).astype(BF16)
    b_dec2 = b_dec[0].reshape(1, 2 * D_KEY)

    n_rows = -(-(bsz + 1) // 8) * 8
    c_rows = jnp.zeros((n_rows, D_MODEL), F32).at[:bsz].set(c).at[bsz].set(c_ctx)
    mod = _ada_call(c_rows, w_ada[0], b_ada).reshape(n_rows, 6, D_MODEL)

    g_mix2 = g_mix.reshape(1, D_MODEL)
    s_cf, s_cb = _ctx_call(ctx, mod, g_mix2, w_qkvg[:, D_KEY:2 * D_KEY],
                           w_qkvg[:, 2 * D_KEY:2 * D_KEY + D_VAL], w_dd, w_dec, b_dec2, bsz)
    qf, kf, v, ob, gs, sga, mb, fst = _proj_bwd_call(
        x, mod, g_mix2, w_qkvg, w_dd, w_dec, b_dec2, w_c3, w_gate, w_conv[0],
        w_conv_out[0].astype(BF16), s_cb)
    x1 = _mix_fwd_call(x, mod, qf, kf, v, ob, gs, sga, mb, fst, g_head.reshape(1, HEAD_V),
                       w_gla_out[0].astype(BF16), w_mix_out[0].astype(BF16), s_cf)
    return _ffn_call(x1, mod, g_ffn.reshape(1, D_MODEL), w_ffn_in[0].astype(BF16),
                     w_ffn_out[0].astype(BF16), g_final.reshape(1, D_MODEL))
```

```python
import functools

import jax
import jax.numpy as jnp
from jax import lax
from jax.experimental import pallas as pl
from jax.experimental.pallas import tpu as pltpu

D_MODEL = 1024
N_HEADS = 4
D_KEY = 512
D_VAL = 1024
HEAD_K = D_KEY // N_HEADS
HEAD_V = D_VAL // N_HEADS
DECAY_RANK = 16
DECAY_TAU = 16.0
CHUNK = 64
GRID_W = 64
D_FF = 2816
EPS = 1e-6

TM_PROJ = 256
TM_FFN = 1024
FFN_SUB = 2
MXU_TILE = 256
FF_GROUPS = (6 * MXU_TILE, 5 * MXU_TILE)
VMEM_LIMIT = 60 * 1024 * 1024

BF16 = jnp.bfloat16
F32 = jnp.float32

NT_DIMS = (((1,), (1,)), ((), ()))
TN_DIMS = (((0,), (0,)), ((), ()))


def _dot(a, b):
    return jnp.dot(a, b, preferred_element_type=F32)


def _sigmoid(x):
    return 1.0 / (1.0 + jnp.exp(-x))


def _silu(x):
    return x * _sigmoid(x)


def _log_sigmoid(x):
    return jnp.minimum(x, 0.0) - jnp.log(1.0 + jnp.exp(-jnp.abs(x)))


def _rms(x):
    return x * lax.rsqrt(jnp.mean(x * x, axis=-1, keepdims=True) + EPS)


def _cumsum_matrix(n, upper):
    r = lax.broadcasted_iota(jnp.int32, (n, n), 0)
    c = lax.broadcasted_iota(jnp.int32, (n, n), 1)
    same = (r // CHUNK) == (c // CHUNK)
    tri = (c >= r) if upper else (c <= r)
    return jnp.where(same & tri, 1.0, 0.0).astype(BF16)


def _chunk_cumsum(mat, la):
    hi = la.astype(BF16)
    lo = (la - hi.astype(F32)).astype(BF16)
    return _dot(mat, hi) + _dot(mat, lo)


def _log_decay(hb, wdd_ref, wdec_ref, bdec_ref):
    dd = _dot(hb, wdd_ref[...]).astype(BF16)
    zd = _dot(dd, wdec_ref[...]) + bdec_ref[...]
    return _log_sigmoid(zd) * (1.0 / DECAY_TAU)


def _kv_all_heads(kc, vc):
    v_stack = jnp.concatenate([vc[:, h * HEAD_V:(h + 1) * HEAD_V] for h in range(N_HEADS)], axis=0)
    r = lax.broadcasted_iota(jnp.int32, (N_HEADS * CHUNK, D_KEY), 0) // CHUNK
    c = lax.broadcasted_iota(jnp.int32, (N_HEADS * CHUNK, D_KEY), 1) // HEAD_K
    k_blk = jnp.where(r == c, jnp.concatenate([kc] * N_HEADS, axis=0), jnp.zeros((), kc.dtype))
    return lax.dot_general(v_stack, k_blk, TN_DIMS, preferred_element_type=F32)


def _no_fill():
    pass


def _gla_chunk(qc, kc, vc, st_ref, e_mid, e_last, e_gap, upper, fill=_no_fill):
    r = lax.broadcasted_iota(jnp.int32, (CHUNK, CHUNK), 0)
    c = lax.broadcasted_iota(jnp.int32, (CHUNK, CHUNK), 1)
    keep = (c >= r) if upper else (c <= r)
    heads = [(slice(h * HEAD_K, (h + 1) * HEAD_K), slice(h * HEAD_V, (h + 1) * HEAD_V))
             for h in range(N_HEADS)]
    scores = [lax.dot_general(qc[:, ks], kc[:, ks], NT_DIMS, preferred_element_type=F32)
              for ks, _ in heads]
    fill()
    st = st_ref[...]
    st_in = (st * e_mid).astype(BF16)
    outs = []
    for (ks, vs), s in zip(heads, scores):
        a = jnp.where(keep, s, 0.0).astype(BF16)
        outs.append(_dot(a, vc[:, vs])
                    + lax.dot_general(qc[:, ks], st_in[:, ks], NT_DIMS, preferred_element_type=F32))
    fill()
    st_ref[...] = st * e_last + _kv_all_heads(kc, vc) * e_gap
    fill()
    return jnp.concatenate(outs, axis=-1)


def _state_update_only(kc, vc, st_ref, e_last, e_gap):
    st_ref[...] = st_ref[...] * e_last + _kv_all_heads(kc, vc) * e_gap


def _ada_kernel(c_ref, w_ref, b_ref, o_ref):
    s = _silu(c_ref[...]).astype(BF16)
    o_ref[...] = _dot(s, w_ref[...].astype(BF16)) + b_ref[...]


def _ada_call(c_rows, w_ada, b_ada):
    n_rows = c_rows.shape[0]
    n_out = w_ada.shape[1]
    bn = 512
    return pl.pallas_call(
        _ada_kernel,
        grid=(n_out // bn,),
        in_specs=[
            pl.BlockSpec((n_rows, D_MODEL), lambda j: (0, 0)),
            pl.BlockSpec((D_MODEL, bn), lambda j: (0, j)),
            pl.BlockSpec((1, bn), lambda j: (0, j)),
        ],
        out_specs=pl.BlockSpec((n_rows, bn), lambda j: (0, j)),
        out_shape=jax.ShapeDtypeStruct((n_rows, n_out), F32),
        name="ada",
    )(c_rows, w_ada, b_ada)


def _ctx_kernel(x_ref, mod_ref, g_ref, wk_ref, wv_ref, wdd_ref, wdec_ref, bdec_ref,
                sf_ref, sb_ref):
    n = x_ref.shape[1]
    nch = n // CHUNK
    x = x_ref[0]
    h = _rms(x) * g_ref[...] * (1.0 + mod_ref[0, 1:2, :]) + mod_ref[0, 0:1, :]
    hb = h.astype(BF16)
    la = _log_decay(hb, wdd_ref, wdec_ref, bdec_ref)
    b_f = _chunk_cumsum(_cumsum_matrix(n, False), la[:, :D_KEY])
    b_b = _chunk_cumsum(_cumsum_matrix(n, True), la[:, D_KEY:])
    k = _dot(hb, wk_ref[...])
    v = _dot(hb, wv_ref[...]).astype(BF16)
    sf_ref[0] = jnp.zeros(sf_ref.shape[1:], F32)
    sb_ref[0] = jnp.zeros(sb_ref.shape[1:], F32)
    mid = CHUNK // 2
    for ci in range(nch):
        rows = slice(ci * CHUNK, (ci + 1) * CHUNK)
        bc = b_f[rows]
        b_mid, b_last = bc[mid - 1:mid], bc[CHUNK - 1:CHUNK]
        kc = (k[rows] * jnp.exp(b_mid - bc)).astype(BF16)
        _state_update_only(kc, v[rows], sf_ref.at[0], jnp.exp(b_last), jnp.exp(b_last - b_mid))
    for ci in reversed(range(nch)):
        rows = slice(ci * CHUNK, (ci + 1) * CHUNK)
        bc = b_b[rows]
        b_mid, b_last = bc[mid:mid + 1], bc[0:1]
        kc = (k[rows] * jnp.exp(b_mid - bc)).astype(BF16)
        _state_update_only(kc, v[rows], sb_ref.at[0], jnp.exp(b_last), jnp.exp(b_last - b_mid))


def _ctx_call(ctx, mod, g_mix, w_k, w_v, w_dd, w_dec, b_dec, ctx_row):
    bsz, n, _ = ctx.shape
    const = lambda b: (0, 0)
    st_spec = pl.BlockSpec((1, HEAD_V, D_KEY), lambda b: (b, 0, 0))
    st_shape = jax.ShapeDtypeStruct((bsz, HEAD_V, D_KEY), F32)
    return pl.pallas_call(
        _ctx_kernel,
        grid=(bsz,),
        in_specs=[
            pl.BlockSpec((1, n, D_MODEL), lambda b: (b, 0, 0)),
            pl.BlockSpec((1, 6, D_MODEL), lambda b: (ctx_row, 0, 0)),
            pl.BlockSpec((1, D_MODEL), const),
            pl.BlockSpec(w_k.shape, const),
            pl.BlockSpec(w_v.shape, const),
            pl.BlockSpec(w_dd.shape, const),
            pl.BlockSpec(w_dec.shape, const),
            pl.BlockSpec(b_dec.shape, const),
        ],
        out_specs=[st_spec, st_spec],
        out_shape=[st_shape, st_shape],
        compiler_params=pltpu.CompilerParams(vmem_limit_bytes=VMEM_LIMIT),
        name="ctx_states",
    )(ctx, mod, g_mix, w_k, w_v, w_dd, w_dec, b_dec)


def _proj_bwd_kernel(x_ref, mod_ref, g_ref, wqkvg_ref, wdd_ref, wdec_ref, bdec_ref, wc3_ref,
                     wgate_ref, wconv_ref, wco_ref, s0_ref,
                     qf_ref, kf_ref, v_ref, ob_ref, gs_ref, sga_ref, mb_ref, fst_ref,
                     st_ref, p_ref):
    tm = x_ref.shape[1]
    nch = tm // CHUNK
    mid = CHUNK // 2

    @pl.when(pl.program_id(1) == 0)
    def _():
        st_ref[...] = s0_ref[0]

    x = x_ref[0]
    h = _rms(x) * g_ref[...] * (1.0 + mod_ref[0, 1:2, :]) + mod_ref[0, 0:1, :]
    hb = h.astype(BF16)

    half = D_MODEL // 2
    halves = (slice(0, half), slice(half, D_MODEL))
    o_g = 2 * D_KEY + D_VAL
    pos = lax.broadcasted_iota(jnp.int32, (tm, 1), 0) % GRID_W
    tmp = {}

    def gate_item(cols):
        def run():
            g = _dot(hb, wqkvg_ref[:, o_g + cols.start:o_g + cols.stop])
            gs_ref[0, :, cols] = _silu(g).astype(BF16)
        return run

    def conv_in_item(name, idx, cols):
        def run():
            tmp[name] = _dot(hb, wc3_ref[:, idx * D_MODEL + cols.start:idx * D_MODEL + cols.stop])
        return run

    def conv_item(cols):
        def run():
            cb = _dot(hb, wc3_ref[:, cols])
            u = tmp.pop("cc") * tmp.pop("ch")
            u_prev = jnp.where(pos == 0, 0.0, pltpu.roll(u, 1, 0))
            u_next = jnp.where(pos == GRID_W - 1, 0.0, pltpu.roll(u, tm - 1, 0))
            uc = (u_prev * wconv_ref[0:1, cols] + u * wconv_ref[1:2, cols]
                  + u_next * wconv_ref[2:3, cols])
            p_ref[:, cols] = (cb * uc).astype(BF16)
        return run

    def conv_out_item(cols):
        def run():
            tmp["yc"] = _dot(p_ref[...], wco_ref[:, cols])
        return run

    def merge_gate_item(cols):
        def run():
            gb = _dot(hb, wgate_ref[:, D_MODEL + cols.start:D_MODEL + cols.stop])
            mb_ref[0, :, cols] = (_sigmoid(gb) * tmp.pop("yc")).astype(BF16)
        return run

    def gla_gate_item(cols):
        def run():
            ga = _dot(hb, wgate_ref[:, cols])
            sga_ref[0, :, cols] = _sigmoid(ga).astype(BF16)
        return run

    items = [gate_item(c) for c in halves]
    for c in halves:
        items += [conv_in_item("cc", 1, c), conv_in_item("ch", 2, c), conv_item(c)]
    for c in halves:
        items += [conv_out_item(c), merge_gate_item(c)]
    items += [gla_gate_item(c) for c in halves]
    items.reverse()

    def fill():
        if items:
            items.pop()()

    la = _log_decay(hb, wdd_ref, wdec_ref, bdec_ref)
    fill()
    b_f = _chunk_cumsum(_cumsum_matrix(tm, False), la[:, :D_KEY])
    b_b = _chunk_cumsum(_cumsum_matrix(tm, True), la[:, D_KEY:])
    fill()

    q = _dot(hb, wqkvg_ref[:, 0:D_KEY]) * (HEAD_K ** -0.5)
    k = _dot(hb, wqkvg_ref[:, D_KEY:2 * D_KEY])
    vb = _dot(hb, wqkvg_ref[:, 2 * D_KEY:2 * D_KEY + D_VAL]).astype(BF16)
    v_ref[0] = vb

    for ci in reversed(range(nch)):
        rows = slice(ci * CHUNK, (ci + 1) * CHUNK)
        bc = b_f[rows]
        b_mid, b_last = bc[mid - 1:mid], bc[CHUNK - 1:CHUNK]
        d = bc - b_mid
        qf_ref[0, rows, :] = (q[rows] * jnp.exp(d)).astype(BF16)
        kf_ref[0, rows, :] = (k[rows] * jnp.exp(-d)).astype(BF16)
        fst_ref[0, 0, ci:ci + 1, :] = jnp.concatenate([b_mid, b_last], axis=-1)
        bc = b_b[rows]
        b_mid, b_last = bc[mid:mid + 1], bc[0:1]
        d = bc - b_mid
        qc = (q[rows] * jnp.exp(d)).astype(BF16)
        kc = (k[rows] * jnp.exp(-d)).astype(BF16)
        ob_ref[0, rows, :] = _gla_chunk(qc, kc, vb[rows], st_ref, jnp.exp(b_mid),
                                        jnp.exp(b_last), jnp.exp(b_last - b_mid), True, fill)

    while items:
        fill()


def _proj_bwd_call(x, mod, g_mix, w_qkvg, w_dd, w_dec, b_dec, w_c3, w_gate, w_conv, w_co, s0):
    bsz, t, _ = x.shape
    tm = TM_PROJ
    nt = t // tm
    nch = tm // CHUNK
    rev = lambda b, i: (b, nt - 1 - i, 0)
    const = lambda b, i: (0, 0)
    tok = lambda width: pl.BlockSpec((1, tm, width), rev)
    tok_shape = lambda width, dt: jax.ShapeDtypeStruct((bsz, t, width), dt)
    return pl.pallas_call(
        _proj_bwd_kernel,
        grid=(bsz, nt),
        in_specs=[
            tok(D_MODEL),
            pl.BlockSpec((1, 6, D_MODEL), lambda b, i: (b, 0, 0)),
            pl.BlockSpec((1, D_MODEL), const),
            pl.BlockSpec(w_qkvg.shape, const),
            pl.BlockSpec(w_dd.shape, const),
            pl.BlockSpec(w_dec.shape, const),
            pl.BlockSpec(b_dec.shape, const),
            pl.BlockSpec(w_c3.shape, const),
            pl.BlockSpec(w_gate.shape, const),
            pl.BlockSpec(w_conv.shape, const),
            pl.BlockSpec(w_co.shape, const),
            pl.BlockSpec((1, HEAD_V, D_KEY), lambda b, i: (b, 0, 0)),
        ],
        out_specs=[
            tok(D_KEY), tok(D_KEY), tok(D_VAL), tok(D_VAL), tok(D_VAL), tok(D_MODEL), tok(D_MODEL),
            pl.BlockSpec((1, 1, nch, 2 * D_KEY), lambda b, i: (b, nt - 1 - i, 0, 0)),
        ],
        out_shape=[
            tok_shape(D_KEY, BF16), tok_shape(D_KEY, BF16), tok_shape(D_VAL, BF16),
            tok_shape(D_VAL, F32), tok_shape(D_VAL, BF16), tok_shape(D_MODEL, BF16),
            tok_shape(D_MODEL, BF16),
            jax.ShapeDtypeStruct((bsz, nt, nch, 2 * D_KEY), F32),
        ],
        scratch_shapes=[pltpu.VMEM((HEAD_V, D_KEY), F32), pltpu.VMEM((tm, D_MODEL), BF16)],
        compiler_params=pltpu.CompilerParams(
            dimension_semantics=("arbitrary", "arbitrary"), vmem_limit_bytes=VMEM_LIMIT),
        name="proj_bwd",
    )(x, mod, g_mix, w_qkvg, w_dd, w_dec, b_dec, w_c3, w_gate, w_conv, w_co, s0)


def _mix_fwd_kernel(x_ref, mod_ref, qf_ref, kf_ref, v_ref, ob_ref, gs_ref, sga_ref, mb_ref,
                    fst_ref, gh_ref, wgo_ref, wmo_ref, s0_ref, o_ref, st_ref, on_ref):
    tm = x_ref.shape[1]
    nch = tm // CHUNK

    @pl.when(pl.program_id(1) == 0)
    def _():
        st_ref[...] = s0_ref[0]

    for ci in range(nch):
        rows = slice(ci * CHUNK, (ci + 1) * CHUNK)
        b_mid = fst_ref[0, 0, ci:ci + 1, 0:D_KEY]
        b_last = fst_ref[0, 0, ci:ci + 1, D_KEY:]
        o = _gla_chunk(qf_ref[0, rows, :], kf_ref[0, rows, :], v_ref[0, rows, :], st_ref,
                       jnp.exp(b_mid), jnp.exp(b_last), jnp.exp(b_last - b_mid), False)
        o = o + ob_ref[0, rows, :]
        for h in range(N_HEADS):
            vs = slice(h * HEAD_V, (h + 1) * HEAD_V)
            on_ref[rows, vs] = _rms(o[:, vs]) * gh_ref[...]

    a = (on_ref[...] * gs_ref[0].astype(F32)).astype(BF16)
    y_gla = _dot(a, wgo_ref[...])
    merged = sga_ref[0].astype(F32) * y_gla + mb_ref[0].astype(F32)
    mix = _dot(merged.astype(BF16), wmo_ref[...])
    o_ref[0] = x_ref[0] + mod_ref[0, 2:3, :] * mix


def _mix_fwd_call(x, mod, qf, kf, v, ob, gs, sga, mb, fst, g_head, w_go, w_mo, s0):
    bsz, t, _ = x.shape
    tm = TM_PROJ
    nt = t // tm
    nch = tm // CHUNK
    fwd = lambda b, i: (b, i, 0)
    const = lambda b, i: (0, 0)
    tok = lambda width: pl.BlockSpec((1, tm, width), fwd)
    return pl.pallas_call(
        _mix_fwd_kernel,
        grid=(bsz, nt),
        in_specs=[
            tok(D_MODEL),
            pl.BlockSpec((1, 6, D_MODEL), lambda b, i: (b, 0, 0)),
            tok(D_KEY), tok(D_KEY), tok(D_VAL), tok(D_VAL), tok(D_VAL), tok(D_MODEL), tok(D_MODEL),
            pl.BlockSpec((1, 1, nch, 2 * D_KEY), lambda b, i: (b, i, 0, 0)),
            pl.BlockSpec((1, HEAD_V), const),
            pl.BlockSpec(w_go.shape, const),
            pl.BlockSpec(w_mo.shape, const),
            pl.BlockSpec((1, HEAD_V, D_KEY), lambda b, i: (b, 0, 0)),
        ],
        out_specs=tok(D_MODEL),
        out_shape=jax.ShapeDtypeStruct((bsz, t, D_MODEL), F32),
        scratch_shapes=[pltpu.VMEM((HEAD_V, D_KEY), F32), pltpu.VMEM((tm, D_VAL), F32)],
        compiler_params=pltpu.CompilerParams(
            dimension_semantics=("arbitrary", "arbitrary"), vmem_limit_bytes=VMEM_LIMIT),
        name="mix_fwd",
    )(x, mod, qf, kf, v, ob, gs, sga, mb, fst, g_head, w_go, w_mo, s0)


def _ffn_kernel(x_ref, mod_ref, g_ref, win_ref, wout_ref, gfin_ref, o_ref):
    sub = x_ref.shape[1] // FFN_SUB
    for s in range(FFN_SUB):
        rows = slice(s * sub, (s + 1) * sub)
        x = x_ref[0, rows, :]
        h = _rms(x) * g_ref[...] * (1.0 + mod_ref[0, 4:5, :]) + mod_ref[0, 3:4, :]
        hb = h.astype(BF16)
        y = None
        start = 0
        for width in FF_GROUPS:
            cols = slice(start, start + width)
            gate_cols = slice(D_FF + start, D_FF + start + width)
            start += width
            t = (_silu(_dot(hb, win_ref[:, cols])) * _dot(hb, win_ref[:, gate_cols])).astype(BF16)
            part = _dot(t, wout_ref[cols, :])
            y = part if y is None else y + part
        x2 = x + mod_ref[0, 5:6, :] * y
        o_ref[0, rows, :] = _rms(x2) * gfin_ref[...]


def _ffn_call(x, mod, g_ffn, w_in, w_out, g_final):
    bsz, t, _ = x.shape
    tm = TM_FFN
    const = lambda b, i: (0, 0)
    tok = pl.BlockSpec((1, tm, D_MODEL), lambda b, i: (b, i, 0))
    return pl.pallas_call(
        _ffn_kernel,
        grid=(bsz, t // tm),
        in_specs=[
            tok,
            pl.BlockSpec((1, 6, D_MODEL), lambda b, i: (b, 0, 0)),
            pl.BlockSpec((1, D_MODEL), const),
            pl.BlockSpec(w_in.shape, const),
            pl.BlockSpec(w_out.shape, const),
            pl.BlockSpec((1, D_MODEL), const),
        ],
        out_specs=tok,
        out_shape=jax.ShapeDtypeStruct((bsz, t, D_MODEL), F32),
        compiler_params=pltpu.CompilerParams(
            dimension_semantics=("arbitrary", "arbitrary"), vmem_limit_bytes=VMEM_LIMIT),
        name="ffn",
    )(x, mod, g_ffn, w_in, w_out, g_final)


def kernel(x, c, ctx, c_ctx, w_ada, b_ada, g_mix, w_in, w_dec_up, b_dec, g_head, w_gla_out,
           w_conv, w_conv_out, w_mix_out, g_ffn, w_ffn_in, w_ffn_out, g_final):
    assert w_ada.shape[0] == 1, "single-layer block"
    bsz = x.shape[0]
    assert x.shape[1] % TM_PROJ == 0 and x.shape[1] % TM_FFN == 0 and ctx.shape[1] % CHUNK == 0

    wi = w_in[0].astype(BF16)
    o_g = 2 * D_KEY + 2 * D_VAL
    o_c = o_g + 2 * DECAY_RANK
    o_gate = o_c + 3 * D_MODEL
    w_qkvg, w_dd, w_c3, w_gate = wi[:, :o_g], wi[:, o_g:o_c], wi[:, o_c:o_gate], wi[:, o_gate:]
    zeros = jnp.zeros((DECAY_RANK, D_KEY), F32)
    w_dec = jnp.concatenate([jnp.concatenate([w_dec_up[0, 0], zeros], axis=1),
                             jnp.concatenate([zeros, w_dec_up[0, 1]], axis=1)], axis=0).astype(BF16)
    b_dec2 = b_dec[0].reshape(1, 2 * D_KEY)

    n_rows = -(-(bsz + 1) // 8) * 8
    c_rows = jnp.zeros((n_rows, D_MODEL), F32).at[:bsz].set(c).at[bsz].set(c_ctx)
    mod = _ada_call(c_rows, w_ada[0], b_ada).reshape(n_rows, 6, D_MODEL)

    g_mix2 = g_mix.reshape(1, D_MODEL)
    s_cf, s_cb = _ctx_call(ctx, mod, g_mix2, w_qkvg[:, D_KEY:2 * D_KEY],
                           w_qkvg[:, 2 * D_KEY:2 * D_KEY + D_VAL], w_dd, w_dec, b_dec2, bsz)
    qf, kf, v, ob, gs, sga, mb, fst = _proj_bwd_call(
        x, mod, g_mix2, w_qkvg, w_dd, w_dec, b_dec2, w_c3, w_gate, w_conv[0],
        w_conv_out[0].astype(BF16), s_cb)
    x1 = _mix_fwd_call(x, mod, qf, kf, v, ob, gs, sga, mb, fst, g_head.reshape(1, HEAD_V),
                       w_gla_out[0].astype(BF16), w_mix_out[0].astype(BF16), s_cf)
    return _ffn_call(x1, mod, g_ffn.reshape(1, D_MODEL), w_ffn_in[0].astype(BF16),
                     w_ffn_out[0].astype(BF16), g_final.reshape(1, D_MODEL))
```

```python
import functools

import jax
import jax.numpy as jnp
from jax import lax
from jax.experimental import pallas as pl
from jax.experimental.pallas import tpu as pltpu

D_MODEL = 1024
N_HEADS = 4
D_KEY = 512
D_VAL = 1024
HEAD_K = D_KEY // N_HEADS
HEAD_V = D_VAL // N_HEADS
DECAY_RANK = 16
DECAY_TAU = 16.0
CHUNK = 64
GRID_W = 64
D_FF = 2816
EPS = 1e-6

MXU_TILE = 256
TM = 256
FF_GROUPS = (512, 512, 512, 512, 512, 256)
VMEM_LIMIT = 60 * 1024 * 1024

BF16 = jnp.bfloat16
F32 = jnp.float32

NT_DIMS = (((1,), (1,)), ((), ()))
TN_DIMS = (((0,), (0,)), ((), ()))


def _dot(a, b):
    return jnp.dot(a, b, preferred_element_type=F32)


def _sigmoid(x):
    return 1.0 / (1.0 + jnp.exp(-x))


def _silu(x):
    return x * _sigmoid(x)


def _log_sigmoid(x):
    return jnp.minimum(x, 0.0) - jnp.log(1.0 + jnp.exp(-jnp.abs(x)))


def _rms(x):
    return x * lax.rsqrt(jnp.mean(x * x, axis=-1, keepdims=True) + EPS)


def _norm_modulate(x, g, shift, scale):
    return (_rms(x) * g * (1.0 + scale) + shift).astype(BF16)


def _cumsum_matrix(n, upper):
    r = lax.broadcasted_iota(jnp.int32, (n, n), 0)
    c = lax.broadcasted_iota(jnp.int32, (n, n), 1)
    same = (r // CHUNK) == (c // CHUNK)
    tri = (c >= r) if upper else (c <= r)
    return jnp.where(same & tri, 1.0, 0.0).astype(BF16)


def _chunk_cumsum(mat, la):
    hi = la.astype(BF16)
    lo = (la - hi.astype(F32)).astype(BF16)
    return _dot(mat, hi) + _dot(mat, lo)


def _log_decay(hb, wdd_ref, wdec_ref, bdec_ref):
    dd = _dot(hb, wdd_ref[...]).astype(BF16)
    zd = _dot(dd, wdec_ref[...]) + bdec_ref[...]
    return _log_sigmoid(zd) * (1.0 / DECAY_TAU)


def _kv_all_heads(kc, vc):
    v_stack = jnp.concatenate([vc[:, h * HEAD_V:(h + 1) * HEAD_V] for h in range(N_HEADS)], axis=0)
    r = lax.broadcasted_iota(jnp.int32, (N_HEADS * CHUNK, D_KEY), 0) // CHUNK
    c = lax.broadcasted_iota(jnp.int32, (N_HEADS * CHUNK, D_KEY), 1) // HEAD_K
    k_blk = jnp.where(r == c, jnp.concatenate([kc] * N_HEADS, axis=0), jnp.zeros((), kc.dtype))
    return lax.dot_general(v_stack, k_blk, TN_DIMS, preferred_element_type=F32)


def _make_fill(items, per_gap):
    queue = list(reversed(items))

    def fill(n=per_gap):
        for _ in range(n):
            if queue:
                queue.pop()()

    def drain():
        while queue:
            queue.pop()()

    return fill, drain


def _no_fill():
    pass


def _gla_chunk(qc, kc, vc, st_ref, e_mid, e_last, e_gap, upper, fill=_no_fill):
    r = lax.broadcasted_iota(jnp.int32, (CHUNK, CHUNK), 0)
    c = lax.broadcasted_iota(jnp.int32, (CHUNK, CHUNK), 1)
    keep = (c >= r) if upper else (c <= r)
    heads = [(slice(h * HEAD_K, (h + 1) * HEAD_K), slice(h * HEAD_V, (h + 1) * HEAD_V))
             for h in range(N_HEADS)]
    scores = [lax.dot_general(qc[:, ks], kc[:, ks], NT_DIMS, preferred_element_type=F32)
              for ks, _ in heads]
    fill()
    st = st_ref[...]
    st_in = (st * e_mid).astype(BF16)
    outs = []
    for (ks, vs), s in zip(heads, scores):
        a = jnp.where(keep, s, 0.0).astype(BF16)
        outs.append(_dot(a, vc[:, vs])
                    + lax.dot_general(qc[:, ks], st_in[:, ks], NT_DIMS, preferred_element_type=F32))
    fill()
    st_ref[...] = st * e_last + _kv_all_heads(kc, vc) * e_gap
    fill()
    return jnp.concatenate(outs, axis=-1)


def _state_update_only(kc, vc, st_ref, e_last, e_gap):
    st_ref[...] = st_ref[...] * e_last + _kv_all_heads(kc, vc) * e_gap


def _ada_kernel(c_ref, w_ref, b_ref, o_ref):
    s = _silu(c_ref[...]).astype(BF16)
    o_ref[...] = _dot(s, w_ref[...].astype(BF16)) + b_ref[...]


def _ada_call(c_rows, w_ada, b_ada):
    n_rows = c_rows.shape[0]
    n_out = w_ada.shape[1]
    bn = 512
    return pl.pallas_call(
        _ada_kernel,
        grid=(n_out // bn,),
        in_specs=[
            pl.BlockSpec((n_rows, D_MODEL), lambda j: (0, 0)),
            pl.BlockSpec((D_MODEL, bn), lambda j: (0, j)),
            pl.BlockSpec((1, bn), lambda j: (0, j)),
        ],
        out_specs=pl.BlockSpec((n_rows, bn), lambda j: (0, j)),
        out_shape=jax.ShapeDtypeStruct((n_rows, n_out), F32),
        name="ada",
    )(c_rows, w_ada, b_ada)


def _ctx_kernel(x_ref, mod_ref, g_ref, wk_ref, wv_ref, wdd_ref, wdec_ref, bdec_ref,
                sf_ref, sb_ref):
    n = x_ref.shape[1]
    nch = n // CHUNK
    hb = _norm_modulate(x_ref[0], g_ref[...], mod_ref[0, 0:1, :], mod_ref[0, 1:2, :])
    la = _log_decay(hb, wdd_ref, wdec_ref, bdec_ref)
    b_f = _chunk_cumsum(_cumsum_matrix(n, False), la[:, :D_KEY])
    b_b = _chunk_cumsum(_cumsum_matrix(n, True), la[:, D_KEY:])
    k = _dot(hb, wk_ref[...])
    v = _dot(hb, wv_ref[...]).astype(BF16)
    sf_ref[0] = jnp.zeros(sf_ref.shape[1:], F32)
    sb_ref[0] = jnp.zeros(sb_ref.shape[1:], F32)
    mid = CHUNK // 2
    for ci in range(nch):
        rows = slice(ci * CHUNK, (ci + 1) * CHUNK)
        bc = b_f[rows]
        b_mid, b_last = bc[mid - 1:mid], bc[CHUNK - 1:CHUNK]
        kc = (k[rows] * jnp.exp(b_mid - bc)).astype(BF16)
        _state_update_only(kc, v[rows], sf_ref.at[0], jnp.exp(b_last), jnp.exp(b_last - b_mid))
    for ci in reversed(range(nch)):
        rows = slice(ci * CHUNK, (ci + 1) * CHUNK)
        bc = b_b[rows]
        b_mid, b_last = bc[mid:mid + 1], bc[0:1]
        kc = (k[rows] * jnp.exp(b_mid - bc)).astype(BF16)
        _state_update_only(kc, v[rows], sb_ref.at[0], jnp.exp(b_last), jnp.exp(b_last - b_mid))


def _ctx_call(ctx, mod, g_mix, w_k, w_v, w_dd, w_dec, b_dec, ctx_row):
    bsz, n, _ = ctx.shape
    const = lambda b: (0, 0)
    st_spec = pl.BlockSpec((1, HEAD_V, D_KEY), lambda b: (b, 0, 0))
    st_shape = jax.ShapeDtypeStruct((bsz, HEAD_V, D_KEY), F32)
    return pl.pallas_call(
        _ctx_kernel,
        grid=(bsz,),
        in_specs=[
            pl.BlockSpec((1, n, D_MODEL), lambda b: (b, 0, 0)),
            pl.BlockSpec((1, 6, D_MODEL), lambda b: (ctx_row, 0, 0)),
            pl.BlockSpec((1, D_MODEL), const),
            pl.BlockSpec(w_k.shape, const),
            pl.BlockSpec(w_v.shape, const),
            pl.BlockSpec(w_dd.shape, const),
            pl.BlockSpec(w_dec.shape, const),
            pl.BlockSpec(b_dec.shape, const),
        ],
        out_specs=[st_spec, st_spec],
        out_shape=[st_shape, st_shape],
        compiler_params=pltpu.CompilerParams(vmem_limit_bytes=VMEM_LIMIT),
        name="ctx_states",
    )(ctx, mod, g_mix, w_k, w_v, w_dd, w_dec, b_dec)


def _proj_bwd_kernel(nt, x0_ref, mod0_ref, xn_ref, modn_ref, g_ref, wqkvg_ref, wdd_ref, wdec_ref,
                     bdec_ref, wc3_ref, wgate_ref, wconv_ref, wco_ref, s0_ref,
                     qf_ref, kf_ref, v_ref, ob_ref, gs_ref, sga_ref, mb_ref, fst_ref,
                     st_ref, p_ref, hb_ref):
    tm = xn_ref.shape[1]
    nch = tm // CHUNK
    mid = CHUNK // 2
    s = pl.program_id(0)
    cur = s % 2

    @pl.when(s == 0)
    def _():
        hb_ref[0] = _norm_modulate(x0_ref[0], g_ref[...], mod0_ref[0, 0:1, :], mod0_ref[0, 1:2, :])

    @pl.when(s % nt == 0)
    def _():
        st_ref[...] = s0_ref[0]

    hb = hb_ref.at[cur]
    half = D_MODEL // 2
    halves = (slice(0, half), slice(half, D_MODEL))
    o_g = 2 * D_KEY + D_VAL
    pos = lax.broadcasted_iota(jnp.int32, (tm, 1), 0) % GRID_W
    tmp = {}

    def gate_item(cols):
        def run():
            g = _dot(hb[...], wqkvg_ref[:, o_g + cols.start:o_g + cols.stop])
            gs_ref[0, :, cols] = _silu(g).astype(BF16)
        return run

    def conv_in_item(name, idx, cols):
        def run():
            tmp[name] = _dot(hb[...], wc3_ref[:, idx * D_MODEL + cols.start:idx * D_MODEL + cols.stop])
        return run

    def conv_item(cols):
        def run():
            cb = _dot(hb[...], wc3_ref[:, cols])
            u = tmp.pop("cc") * tmp.pop("ch")
            u_prev = jnp.where(pos == 0, 0.0, pltpu.roll(u, 1, 0))
            u_next = jnp.where(pos == GRID_W - 1, 0.0, pltpu.roll(u, tm - 1, 0))
            uc = (u_prev * wconv_ref[0:1, cols] + u * wconv_ref[1:2, cols]
                  + u_next * wconv_ref[2:3, cols])
            p_ref[:, cols] = (cb * uc).astype(BF16)
        return run

    def conv_out_item(cols):
        def run():
            tmp["yc"] = _dot(p_ref[...], wco_ref[:, cols])
        return run

    def merge_gate_item(cols):
        def run():
            gb = _dot(hb[...], wgate_ref[:, D_MODEL + cols.start:D_MODEL + cols.stop])
            mb_ref[0, :, cols] = (_sigmoid(gb) * tmp.pop("yc")).astype(BF16)
        return run

    def gla_gate_item(cols):
        def run():
            ga = _dot(hb[...], wgate_ref[:, cols])
            sga_ref[0, :, cols] = _sigmoid(ga).astype(BF16)
        return run

    def next_input_item():
        hb_ref[1 - cur] = _norm_modulate(xn_ref[0], g_ref[...], modn_ref[0, 0:1, :],
                                         modn_ref[0, 1:2, :])

    items = [gate_item(c) for c in halves]
    for c in halves:
        items += [conv_in_item("cc", 1, c), conv_in_item("ch", 2, c), conv_item(c)]
    items += [next_input_item]
    for c in halves:
        items += [conv_out_item(c), merge_gate_item(c)]
    items += [gla_gate_item(c) for c in halves]
    fill, drain = _make_fill(items, 1)

    la = _log_decay(hb[...], wdd_ref, wdec_ref, bdec_ref)
    fill()
    b_f = _chunk_cumsum(_cumsum_matrix(tm, False), la[:, :D_KEY])
    b_b = _chunk_cumsum(_cumsum_matrix(tm, True), la[:, D_KEY:])
    fill()

    q = _dot(hb[...], wqkvg_ref[:, 0:D_KEY]) * (HEAD_K ** -0.5)
    k = _dot(hb[...], wqkvg_ref[:, D_KEY:2 * D_KEY])
    vb = _dot(hb[...], wqkvg_ref[:, 2 * D_KEY:2 * D_KEY + D_VAL]).astype(BF16)
    v_ref[0] = vb

    for ci in reversed(range(nch)):
        rows = slice(ci * CHUNK, (ci + 1) * CHUNK)
        bc = b_f[rows]
        b_mid, b_last = bc[mid - 1:mid], bc[CHUNK - 1:CHUNK]
        d = bc - b_mid
        qf_ref[0, rows, :] = (q[rows] * jnp.exp(d)).astype(BF16)
        kf_ref[0, rows, :] = (k[rows] * jnp.exp(-d)).astype(BF16)
        fst_ref[0, 0, ci:ci + 1, :] = jnp.concatenate([b_mid, b_last], axis=-1)
        bc = b_b[rows]
        b_mid, b_last = bc[mid:mid + 1], bc[0:1]
        d = bc - b_mid
        qc = (q[rows] * jnp.exp(d)).astype(BF16)
        kc = (k[rows] * jnp.exp(-d)).astype(BF16)
        ob_ref[0, rows, :] = _gla_chunk(qc, kc, vb[rows], st_ref, jnp.exp(b_mid),
                                        jnp.exp(b_last), jnp.exp(b_last - b_mid), True, fill)
    drain()


def _proj_bwd_call(x, mod, g_mix, w_qkvg, w_dd, w_dec, b_dec, w_c3, w_gate, w_conv, w_co, s0):
    bsz, t, _ = x.shape
    tm = TM
    nt = t // tm
    nch = tm // CHUNK
    steps = bsz * nt
    cur = lambda s: (s // nt, nt - 1 - s % nt, 0)
    nxt = lambda s: cur(jnp.minimum(s + 1, steps - 1))
    const = lambda s: (0, 0)
    tok = lambda width: pl.BlockSpec((1, tm, width), cur)
    tok_shape = lambda width, dt: jax.ShapeDtypeStruct((bsz, t, width), dt)
    return pl.pallas_call(
        functools.partial(_proj_bwd_kernel, nt),
        grid=(steps,),
        in_specs=[
            pl.BlockSpec((1, tm, D_MODEL), lambda s: (0, nt - 1, 0)),
            pl.BlockSpec((1, 6, D_MODEL), lambda s: (0, 0, 0)),
            pl.BlockSpec((1, tm, D_MODEL), nxt),
            pl.BlockSpec((1, 6, D_MODEL), lambda s: (jnp.minimum(s + 1, steps - 1) // nt, 0, 0)),
            pl.BlockSpec((1, D_MODEL), const),
            pl.BlockSpec(w_qkvg.shape, const),
            pl.BlockSpec(w_dd.shape, const),
            pl.BlockSpec(w_dec.shape, const),
            pl.BlockSpec(b_dec.shape, const),
            pl.BlockSpec(w_c3.shape, const),
            pl.BlockSpec(w_gate.shape, const),
            pl.BlockSpec(w_conv.shape, const),
            pl.BlockSpec(w_co.shape, const),
            pl.BlockSpec((1, HEAD_V, D_KEY), lambda s: (s // nt, 0, 0)),
        ],
        out_specs=[
            tok(D_KEY), tok(D_KEY), tok(D_VAL), tok(D_VAL), tok(D_VAL), tok(D_MODEL), tok(D_MODEL),
            pl.BlockSpec((1, 1, nch, 2 * D_KEY), lambda s: (s // nt, nt - 1 - s % nt, 0, 0)),
        ],
        out_shape=[
            tok_shape(D_KEY, BF16), tok_shape(D_KEY, BF16), tok_shape(D_VAL, BF16),
            tok_shape(D_VAL, F32), tok_shape(D_VAL, BF16), tok_shape(D_MODEL, BF16),
            tok_shape(D_MODEL, BF16),
            jax.ShapeDtypeStruct((bsz, nt, nch, 2 * D_KEY), F32),
        ],
        scratch_shapes=[pltpu.VMEM((HEAD_V, D_KEY), F32), pltpu.VMEM((tm, D_MODEL), BF16),
                        pltpu.VMEM((2, tm, D_MODEL), BF16)],
        compiler_params=pltpu.CompilerParams(
            dimension_semantics=("arbitrary",), vmem_limit_bytes=VMEM_LIMIT),
        name="proj_bwd",
    )(x, mod, x, mod, g_mix, w_qkvg, w_dd, w_dec, b_dec, w_c3, w_gate, w_conv, w_co, s0)


def _fwd_ffn_kernel(nt, steps, qf_ref, kf_ref, v_ref, ob_ref, gs_ref, fst_ref, s0_ref, gh_ref,
                    x_ref, sga_ref, mb_ref, mod_ref, wgo_ref, wmo_ref, gffn_ref, win_ref, wout_ref,
                    gfin_ref, o_ref, st_ref, a_ref):
    tm = x_ref.shape[1]
    nch = tm // CHUNK
    s = pl.program_id(0)
    t = jnp.minimum(s, steps - 1)

    @pl.when(s == 0)
    def _():
        a_ref[...] = jnp.zeros(a_ref.shape, a_ref.dtype)

    @pl.when(t % nt == 0)
    def _():
        st_ref[...] = s0_ref[0]

    tmp = {}

    def mix_item():
        merged = sga_ref[0].astype(F32) * tmp.pop("y_gla") + mb_ref[0].astype(F32)
        mix = _dot(merged.astype(BF16), wmo_ref[...])
        x1 = x_ref[0] + mod_ref[0, 2:3, :] * mix
        tmp["x1"] = x1
        tmp["hb"] = _norm_modulate(x1, gffn_ref[...], mod_ref[0, 3:4, :], mod_ref[0, 4:5, :])

    def ffn_in_item(name, cols):
        def run():
            tmp[name] = _dot(tmp["hb"], win_ref[:, cols])
        return run

    def ffn_out_item(cols):
        def run():
            hidden = (_silu(tmp.pop("a")) * tmp.pop("b")).astype(BF16)
            part = _dot(hidden, wout_ref[cols, :])
            tmp["y"] = part if "y" not in tmp else tmp["y"] + part
        return run

    items = [mix_item]
    start = 0
    for width in FF_GROUPS:
        cols = slice(start, start + width)
        gate_cols = slice(D_FF + start, D_FF + start + width)
        items += [ffn_in_item("a", cols), ffn_in_item("b", gate_cols), ffn_out_item(cols)]
        start += width
    fill, drain = _make_fill(items, 2)

    tmp["y_gla"] = _dot(a_ref[...], wgo_ref[...])

    for ci in range(nch):
        rows = slice(ci * CHUNK, (ci + 1) * CHUNK)
        b_mid = fst_ref[0, 0, ci:ci + 1, 0:D_KEY]
        b_last = fst_ref[0, 0, ci:ci + 1, D_KEY:]
        o = _gla_chunk(qf_ref[0, rows, :], kf_ref[0, rows, :], v_ref[0, rows, :], st_ref,
                       jnp.exp(b_mid), jnp.exp(b_last), jnp.exp(b_last - b_mid), False, fill)
        o = o + ob_ref[0, rows, :]
        for h in range(N_HEADS):
            vs = slice(h * HEAD_V, (h + 1) * HEAD_V)
            a_ref[rows, vs] = (_rms(o[:, vs]) * gh_ref[...]
                               * gs_ref[0, rows, vs].astype(F32)).astype(BF16)
    drain()

    x2 = tmp["x1"] + mod_ref[0, 5:6, :] * tmp["y"]
    o_ref[0] = _rms(x2) * gfin_ref[...]


def _fwd_ffn_call(x, mod, qf, kf, v, ob, gs, sga, mb, fst, g_head, w_go, w_mo, s0, g_ffn, w_in,
                  w_out, g_final):
    bsz, t, _ = x.shape
    tm = TM
    nt = t // tm
    nch = tm // CHUNK
    steps = bsz * nt
    scan = lambda s: (jnp.minimum(s, steps - 1) // nt, jnp.minimum(s, steps - 1) % nt, 0)
    rest = lambda s: (jnp.maximum(s - 1, 0) // nt, jnp.maximum(s - 1, 0) % nt, 0)
    const = lambda s: (0, 0)
    scan_tok = lambda width: pl.BlockSpec((1, tm, width), scan)
    rest_tok = lambda width: pl.BlockSpec((1, tm, width), rest)
    return pl.pallas_call(
        functools.partial(_fwd_ffn_kernel, nt, steps),
        grid=(steps + 1,),
        in_specs=[
            scan_tok(D_KEY), scan_tok(D_KEY), scan_tok(D_VAL), scan_tok(D_VAL), scan_tok(D_VAL),
            pl.BlockSpec((1, 1, nch, 2 * D_KEY), lambda s: scan(s) + (0,)),
            pl.BlockSpec((1, HEAD_V, D_KEY), lambda s: (jnp.minimum(s, steps - 1) // nt, 0, 0)),
            pl.BlockSpec((1, HEAD_V), const),
            rest_tok(D_MODEL), rest_tok(D_MODEL), rest_tok(D_MODEL),
            pl.BlockSpec((1, 6, D_MODEL), lambda s: (jnp.maximum(s - 1, 0) // nt, 0, 0)),
            pl.BlockSpec(w_go.shape, const),
            pl.BlockSpec(w_mo.shape, const),
            pl.BlockSpec((1, D_MODEL), const),
            pl.BlockSpec(w_in.shape, const),
            pl.BlockSpec(w_out.shape, const),
            pl.BlockSpec((1, D_MODEL), const),
        ],
        out_specs=rest_tok(D_MODEL),
        out_shape=jax.ShapeDtypeStruct((bsz, t, D_MODEL), F32),
        scratch_shapes=[pltpu.VMEM((HEAD_V, D_KEY), F32), pltpu.VMEM((tm, D_VAL), BF16)],
        compiler_params=pltpu.CompilerParams(
            dimension_semantics=("arbitrary",), vmem_limit_bytes=VMEM_LIMIT),
        name="fwd_ffn",
    )(qf, kf, v, ob, gs, fst, s0, g_head, x, sga, mb, mod, w_go, w_mo, g_ffn, w_in, w_out, g_final)


def kernel(x, c, ctx, c_ctx, w_ada, b_ada, g_mix, w_in, w_dec_up, b_dec, g_head, w_gla_out,
           w_conv, w_conv_out, w_mix_out, g_ffn, w_ffn_in, w_ffn_out, g_final):
    assert w_ada.shape[0] == 1, "single-layer block"
    bsz = x.shape[0]
    assert x.shape[1] % TM == 0 and ctx.shape[1] % CHUNK == 0 and sum(FF_GROUPS) == D_FF

    o_g = 2 * D_KEY + 2 * D_VAL
    o_c = o_g + 2 * DECAY_RANK
    o_gate = o_c + 3 * D_MODEL
    piece = lambda lo, hi: w_in[0, :, lo:hi].astype(BF16)
    w_qkvg, w_dd, w_c3, w_gate = piece(0, o_g), piece(o_g, o_c), piece(o_c, o_gate), piece(o_gate, None)
    zeros = jnp.zeros((DECAY_RANK, D_KEY), F32)
    w_dec = jnp.concatenate([jnp.concatenate([w_dec_up[0, 0], zeros], axis=1),
                             jnp.concatenate([zeros, w_dec_up[0, 1]], axis=1)], axis=0).astype(BF16)
    b_dec2 = b_dec[0].reshape(1, 2 * D_KEY)

    n_rows = -(-(bsz + 1) // 8) * 8
    c_rows = jnp.zeros((n_rows, D_MODEL), F32).at[:bsz].set(c).at[bsz].set(c_ctx)
    mod = _ada_call(c_rows, w_ada[0], b_ada).reshape(n_rows, 6, D_MODEL)

    g_mix2 = g_mix.reshape(1, D_MODEL)
    s_cf, s_cb = _ctx_call(ctx, mod, g_mix2, piece(D_KEY, 2 * D_KEY), piece(2 * D_KEY, 2 * D_KEY + D_VAL),
                           w_dd, w_dec, b_dec2, bsz)
    qf, kf, v, ob, gs, sga, mb, fst = _proj_bwd_call(
        x, mod, g_mix2, w_qkvg, w_dd, w_dec, b_dec2, w_c3, w_gate, w_conv[0],
        w_conv_out[0].astype(BF16), s_cb)
    return _fwd_ffn_call(x, mod, qf, kf, v, ob, gs, sga, mb, fst, g_head.reshape(1, HEAD_V),
                         w_gla_out[0].astype(BF16), w_mix_out[0].astype(BF16), s_cf,
                         g_ffn.reshape(1, D_MODEL), w_ffn_in[0].astype(BF16),
                         w_ffn_out[0].astype(BF16), g_final.reshape(1, D_MODEL))
```

```python
import functools

import jax
import jax.numpy as jnp
from jax import lax
from jax.experimental import pallas as pl
from jax.experimental.pallas import tpu as pltpu

D_MODEL = 1024
N_HEADS = 4
D_KEY = 512
D_VAL = 1024
HEAD_K = D_KEY // N_HEADS
HEAD_V = D_VAL // N_HEADS
DECAY_RANK = 16
DECAY_TAU = 16.0
CHUNK = 64
GRID_W = 64
D_FF = 2816
EPS = 1e-6

MXU_TILE = 256
TM_REV = 512
TM_FWD = 512
PROJ_PARTS = 4
FF_GROUPS = (512, 512, 512, 512, 512, 256)
VMEM_LIMIT = 60 * 1024 * 1024

BF16 = jnp.bfloat16
F32 = jnp.float32

NT_DIMS = (((1,), (1,)), ((), ()))
TN_DIMS = (((0,), (0,)), ((), ()))


def _dot(a, b):
    return jnp.dot(a, b, preferred_element_type=F32)


def _sigmoid(x):
    return 1.0 / (1.0 + jnp.exp(-x))


def _silu(x):
    return x * _sigmoid(x)


def _log_sigmoid(x):
    return jnp.minimum(x, 0.0) - jnp.log(1.0 + jnp.exp(-jnp.abs(x)))


def _rms(x):
    return x * lax.rsqrt(jnp.mean(x * x, axis=-1, keepdims=True) + EPS)


def _norm_modulate(x, g, shift, scale):
    return (_rms(x) * g * (1.0 + scale) + shift).astype(BF16)


def _cumsum_matrix(n, upper):
    r = lax.broadcasted_iota(jnp.int32, (n, n), 0)
    c = lax.broadcasted_iota(jnp.int32, (n, n), 1)
    same = (r // CHUNK) == (c // CHUNK)
    tri = (c >= r) if upper else (c <= r)
    return jnp.where(same & tri, 1.0, 0.0).astype(BF16)


def _chunk_cumsum(la, upper):
    rows = min(la.shape[0], MXU_TILE)
    mat = _cumsum_matrix(rows, upper)
    hi = la.astype(BF16)
    lo = (la - hi.astype(F32)).astype(BF16)
    parts = [_dot(mat, hi[r:r + rows]) + _dot(mat, lo[r:r + rows])
             for r in range(0, la.shape[0], rows)]
    return parts[0] if len(parts) == 1 else jnp.concatenate(parts, axis=0)


def _log_decay(hb, wdd_ref, wdec_ref, bdec_ref):
    dd = _dot(hb, wdd_ref[...]).astype(BF16)
    zd = _dot(dd, wdec_ref[...]) + bdec_ref[...]
    return _log_sigmoid(zd) * (1.0 / DECAY_TAU)


def _kv_all_heads(kc, vc):
    v_stack = jnp.concatenate([vc[:, h * HEAD_V:(h + 1) * HEAD_V] for h in range(N_HEADS)], axis=0)
    r = lax.broadcasted_iota(jnp.int32, (N_HEADS * CHUNK, D_KEY), 0) // CHUNK
    c = lax.broadcasted_iota(jnp.int32, (N_HEADS * CHUNK, D_KEY), 1) // HEAD_K
    k_blk = jnp.where(r == c, jnp.concatenate([kc] * N_HEADS, axis=0), jnp.zeros((), kc.dtype))
    return lax.dot_general(v_stack, k_blk, TN_DIMS, preferred_element_type=F32)


def _make_fill(items, per_gap):
    queue = list(reversed(items))

    def fill(n=per_gap):
        for _ in range(n):
            if queue:
                queue.pop()()

    def drain():
        while queue:
            queue.pop()()

    return fill, drain


def _no_fill():
    pass


def _gla_chunk(qc, kc, vc, st_ref, e_mid, e_last, e_gap, upper, fill=_no_fill):
    r = lax.broadcasted_iota(jnp.int32, (CHUNK, CHUNK), 0)
    c = lax.broadcasted_iota(jnp.int32, (CHUNK, CHUNK), 1)
    keep = (c >= r) if upper else (c <= r)
    heads = [(slice(h * HEAD_K, (h + 1) * HEAD_K), slice(h * HEAD_V, (h + 1) * HEAD_V))
             for h in range(N_HEADS)]
    scores = [lax.dot_general(qc[:, ks], kc[:, ks], NT_DIMS, preferred_element_type=F32)
              for ks, _ in heads]
    fill()
    st = st_ref[...]
    st_in = (st * e_mid).astype(BF16)
    outs = []
    for (ks, vs), s in zip(heads, scores):
        a = jnp.where(keep, s, 0.0).astype(BF16)
        outs.append(_dot(a, vc[:, vs])
                    + lax.dot_general(qc[:, ks], st_in[:, ks], NT_DIMS, preferred_element_type=F32))
    fill()
    st_ref[...] = st * e_last + _kv_all_heads(kc, vc) * e_gap
    fill()
    return jnp.concatenate(outs, axis=-1)


def _state_update_only(kc, vc, st_ref, e_last, e_gap):
    st_ref[...] = st_ref[...] * e_last + _kv_all_heads(kc, vc) * e_gap


def _ada_kernel(c_ref, w_ref, b_ref, o_ref):
    s = _silu(c_ref[...]).astype(BF16)
    o_ref[...] = _dot(s, w_ref[...].astype(BF16)) + b_ref[...]


def _ada_call(c_rows, w_ada, b_ada):
    n_rows = c_rows.shape[0]
    n_out = w_ada.shape[1]
    bn = 512
    return pl.pallas_call(
        _ada_kernel,
        grid=(n_out // bn,),
        in_specs=[
            pl.BlockSpec((n_rows, D_MODEL), lambda j: (0, 0)),
            pl.BlockSpec((D_MODEL, bn), lambda j: (0, j)),
            pl.BlockSpec((1, bn), lambda j: (0, j)),
        ],
        out_specs=pl.BlockSpec((n_rows, bn), lambda j: (0, j)),
        out_shape=jax.ShapeDtypeStruct((n_rows, n_out), F32),
        name="ada",
    )(c_rows, w_ada, b_ada)


def _ctx_kernel(x_ref, mod_ref, g_ref, wk_ref, wv_ref, wdd_ref, wdec_ref, bdec_ref,
                sf_ref, sb_ref):
    n = x_ref.shape[1]
    nch = n // CHUNK
    hb = _norm_modulate(x_ref[0], g_ref[...], mod_ref[0, 0:1, :], mod_ref[0, 1:2, :])
    la = _log_decay(hb, wdd_ref, wdec_ref, bdec_ref)
    b_f = _chunk_cumsum(la[:, :D_KEY], False)
    b_b = _chunk_cumsum(la[:, D_KEY:], True)
    k = _dot(hb, wk_ref[...])
    v = _dot(hb, wv_ref[...]).astype(BF16)
    sf_ref[0] = jnp.zeros(sf_ref.shape[1:], F32)
    sb_ref[0] = jnp.zeros(sb_ref.shape[1:], F32)
    mid = CHUNK // 2
    for ci in range(nch):
        rows = slice(ci * CHUNK, (ci + 1) * CHUNK)
        bc = b_f[rows]
        b_mid, b_last = bc[mid - 1:mid], bc[CHUNK - 1:CHUNK]
        kc = (k[rows] * jnp.exp(b_mid - bc)).astype(BF16)
        _state_update_only(kc, v[rows], sf_ref.at[0], jnp.exp(b_last), jnp.exp(b_last - b_mid))
    for ci in reversed(range(nch)):
        rows = slice(ci * CHUNK, (ci + 1) * CHUNK)
        bc = b_b[rows]
        b_mid, b_last = bc[mid:mid + 1], bc[0:1]
        kc = (k[rows] * jnp.exp(b_mid - bc)).astype(BF16)
        _state_update_only(kc, v[rows], sb_ref.at[0], jnp.exp(b_last), jnp.exp(b_last - b_mid))


def _ctx_call(ctx, mod, g_mix, w_k, w_v, w_dd, w_dec, b_dec, ctx_row):
    bsz, n, _ = ctx.shape
    const = lambda b: (0, 0)
    st_spec = pl.BlockSpec((1, HEAD_V, D_KEY), lambda b: (b, 0, 0))
    st_shape = jax.ShapeDtypeStruct((bsz, HEAD_V, D_KEY), F32)
    return pl.pallas_call(
        _ctx_kernel,
        grid=(bsz,),
        in_specs=[
            pl.BlockSpec((1, n, D_MODEL), lambda b: (b, 0, 0)),
            pl.BlockSpec((1, 6, D_MODEL), lambda b: (ctx_row, 0, 0)),
            pl.BlockSpec((1, D_MODEL), const),
            pl.BlockSpec(w_k.shape, const),
            pl.BlockSpec(w_v.shape, const),
            pl.BlockSpec(w_dd.shape, const),
            pl.BlockSpec(w_dec.shape, const),
            pl.BlockSpec(b_dec.shape, const),
        ],
        out_specs=[st_spec, st_spec],
        out_shape=[st_shape, st_shape],
        compiler_params=pltpu.CompilerParams(vmem_limit_bytes=VMEM_LIMIT),
        name="ctx_states",
    )(ctx, mod, g_mix, w_k, w_v, w_dd, w_dec, b_dec)


def _proj_bwd_kernel(nt, x0_ref, mod0_ref, xn_ref, modn_ref, g_ref, wqkvg_ref, wdd_ref, wdec_ref,
                     bdec_ref, wc3_ref, wgate_ref, wconv_ref, wco_ref, s0_ref,
                     qf_ref, kf_ref, v_ref, ob_ref, gs_ref, sga_ref, mb_ref, fst_ref,
                     st_ref, p_ref, hb_ref):
    tm = xn_ref.shape[1]
    nch = tm // CHUNK
    mid = CHUNK // 2
    s = pl.program_id(0)
    cur = s % 2

    @pl.when(s == 0)
    def _():
        hb_ref[0] = _norm_modulate(x0_ref[0], g_ref[...], mod0_ref[0, 0:1, :], mod0_ref[0, 1:2, :])

    @pl.when(s % nt == 0)
    def _():
        st_ref[...] = s0_ref[0]

    hb = hb_ref.at[cur]
    width = D_MODEL // PROJ_PARTS
    halves = tuple(slice(j * width, (j + 1) * width) for j in range(PROJ_PARTS))
    o_g = 2 * D_KEY + D_VAL
    pos = lax.broadcasted_iota(jnp.int32, (tm, 1), 0) % GRID_W
    tmp = {}

    def gate_item(cols):
        def run():
            g = _dot(hb[...], wqkvg_ref[:, o_g + cols.start:o_g + cols.stop])
            gs_ref[0, :, cols] = _silu(g).astype(BF16)
        return run

    def conv_in_item(name, idx, cols):
        def run():
            tmp[name] = _dot(hb[...], wc3_ref[:, idx * D_MODEL + cols.start:idx * D_MODEL + cols.stop])
        return run

    def conv_item(cols):
        def run():
            cb = _dot(hb[...], wc3_ref[:, cols])
            u = tmp.pop("cc") * tmp.pop("ch")
            u_prev = jnp.where(pos == 0, 0.0, pltpu.roll(u, 1, 0))
            u_next = jnp.where(pos == GRID_W - 1, 0.0, pltpu.roll(u, tm - 1, 0))
            uc = (u_prev * wconv_ref[0:1, cols] + u * wconv_ref[1:2, cols]
                  + u_next * wconv_ref[2:3, cols])
            p_ref[:, cols] = (cb * uc).astype(BF16)
        return run

    def conv_out_item(cols):
        def run():
            tmp["yc"] = _dot(p_ref[...], wco_ref[:, cols])
        return run

    def merge_gate_item(cols):
        def run():
            gb = _dot(hb[...], wgate_ref[:, D_MODEL + cols.start:D_MODEL + cols.stop])
            mb_ref[0, :, cols] = (_sigmoid(gb) * tmp.pop("yc")).astype(BF16)
        return run

    def gla_gate_item(cols):
        def run():
            ga = _dot(hb[...], wgate_ref[:, cols])
            sga_ref[0, :, cols] = _sigmoid(ga).astype(BF16)
        return run

    def next_input_item():
        hb_ref[1 - cur] = _norm_modulate(xn_ref[0], g_ref[...], modn_ref[0, 0:1, :],
                                         modn_ref[0, 1:2, :])

    items = [gate_item(c) for c in halves]
    for c in halves:
        items += [conv_in_item("cc", 1, c), conv_in_item("ch", 2, c), conv_item(c)]
    items += [next_input_item]
    for c in halves:
        items += [conv_out_item(c), merge_gate_item(c)]
    items += [gla_gate_item(c) for c in halves]
    fill, drain = _make_fill(items, 1)

    la = _log_decay(hb[...], wdd_ref, wdec_ref, bdec_ref)
    fill()
    b_f = _chunk_cumsum(la[:, :D_KEY], False)
    b_b = _chunk_cumsum(la[:, D_KEY:], True)
    fill()

    q = _dot(hb[...], wqkvg_ref[:, 0:D_KEY]) * (HEAD_K ** -0.5)
    k = _dot(hb[...], wqkvg_ref[:, D_KEY:2 * D_KEY])
    vb = _dot(hb[...], wqkvg_ref[:, 2 * D_KEY:2 * D_KEY + D_VAL]).astype(BF16)
    v_ref[0] = vb

    for ci in reversed(range(nch)):
        rows = slice(ci * CHUNK, (ci + 1) * CHUNK)
        bc = b_f[rows]
        b_mid, b_last = bc[mid - 1:mid], bc[CHUNK - 1:CHUNK]
        d = bc - b_mid
        qf_ref[0, rows, :] = (q[rows] * jnp.exp(d)).astype(BF16)
        kf_ref[0, rows, :] = (k[rows] * jnp.exp(-d)).astype(BF16)
        fst_ref[0, 0, ci:ci + 1, :] = jnp.concatenate([b_mid, b_last], axis=-1)
        bc = b_b[rows]
        b_mid, b_last = bc[mid:mid + 1], bc[0:1]
        d = bc - b_mid
        qc = (q[rows] * jnp.exp(d)).astype(BF16)
        kc = (k[rows] * jnp.exp(-d)).astype(BF16)
        ob_ref[0, rows, :] = _gla_chunk(qc, kc, vb[rows], st_ref, jnp.exp(b_mid),
                                        jnp.exp(b_last), jnp.exp(b_last - b_mid), True, fill)
    drain()


def _proj_bwd_call(x, mod, g_mix, w_qkvg, w_dd, w_dec, b_dec, w_c3, w_gate, w_conv, w_co, s0):
    bsz, t, _ = x.shape
    tm = TM_REV
    nt = t // tm
    nch = tm // CHUNK
    steps = bsz * nt
    cur = lambda s: (s // nt, nt - 1 - s % nt, 0)
    nxt = lambda s: cur(jnp.minimum(s + 1, steps - 1))
    const = lambda s: (0, 0)
    tok = lambda width: pl.BlockSpec((1, tm, width), cur)
    tok_shape = lambda width, dt: jax.ShapeDtypeStruct((bsz, t, width), dt)
    return pl.pallas_call(
        functools.partial(_proj_bwd_kernel, nt),
        grid=(steps,),
        in_specs=[
            pl.BlockSpec((1, tm, D_MODEL), lambda s: (0, nt - 1, 0)),
            pl.BlockSpec((1, 6, D_MODEL), lambda s: (0, 0, 0)),
            pl.BlockSpec((1, tm, D_MODEL), nxt),
            pl.BlockSpec((1, 6, D_MODEL), lambda s: (jnp.minimum(s + 1, steps - 1) // nt, 0, 0)),
            pl.BlockSpec((1, D_MODEL), const),
            pl.BlockSpec(w_qkvg.shape, const),
            pl.BlockSpec(w_dd.shape, const),
            pl.BlockSpec(w_dec.shape, const),
            pl.BlockSpec(b_dec.shape, const),
            pl.BlockSpec(w_c3.shape, const),
            pl.BlockSpec(w_gate.shape, const),
            pl.BlockSpec(w_conv.shape, const),
            pl.BlockSpec(w_co.shape, const),
            pl.BlockSpec((1, HEAD_V, D_KEY), lambda s: (s // nt, 0, 0)),
        ],
        out_specs=[
            tok(D_KEY), tok(D_KEY), tok(D_VAL), tok(D_VAL), tok(D_VAL), tok(D_MODEL), tok(D_MODEL),
            pl.BlockSpec((1, 1, nch, 2 * D_KEY), lambda s: (s // nt, nt - 1 - s % nt, 0, 0)),
        ],
        out_shape=[
            tok_shape(D_KEY, BF16), tok_shape(D_KEY, BF16), tok_shape(D_VAL, BF16),
            tok_shape(D_VAL, F32), tok_shape(D_VAL, BF16), tok_shape(D_MODEL, BF16),
            tok_shape(D_MODEL, BF16),
            jax.ShapeDtypeStruct((bsz, nt, nch, 2 * D_KEY), F32),
        ],
        scratch_shapes=[pltpu.VMEM((HEAD_V, D_KEY), F32), pltpu.VMEM((tm, D_MODEL), BF16),
                        pltpu.VMEM((2, tm, D_MODEL), BF16)],
        compiler_params=pltpu.CompilerParams(
            dimension_semantics=("arbitrary",), vmem_limit_bytes=VMEM_LIMIT),
        name="proj_bwd",
    )(x, mod, x, mod, g_mix, w_qkvg, w_dd, w_dec, b_dec, w_c3, w_gate, w_conv, w_co, s0)


def _fwd_ffn_kernel(nt, steps, qf_ref, kf_ref, v_ref, ob_ref, gs_ref, fst_ref, s0_ref, gh_ref,
                    x_ref, sga_ref, mb_ref, mod_ref, wgo_ref, wmo_ref, gffn_ref, win_ref, wout_ref,
                    gfin_ref, o_ref, st_ref, a_ref):
    tm = x_ref.shape[1]
    nch = tm // CHUNK
    s = pl.program_id(0)
    t = jnp.minimum(s, steps - 1)

    @pl.when(s == 0)
    def _():
        a_ref[...] = jnp.zeros(a_ref.shape, a_ref.dtype)

    @pl.when(t % nt == 0)
    def _():
        st_ref[...] = s0_ref[0]

    tmp = {}

    def mix_item():
        merged = sga_ref[0].astype(F32) * tmp.pop("y_gla") + mb_ref[0].astype(F32)
        mix = _dot(merged.astype(BF16), wmo_ref[...])
        x1 = x_ref[0] + mod_ref[0, 2:3, :] * mix
        tmp["x1"] = x1
        tmp["hb"] = _norm_modulate(x1, gffn_ref[...], mod_ref[0, 3:4, :], mod_ref[0, 4:5, :])

    def ffn_in_item(name, cols):
        def run():
            tmp[name] = _dot(tmp["hb"], win_ref[:, cols])
        return run

    def ffn_out_item(cols):
        def run():
            hidden = (_silu(tmp.pop("a")) * tmp.pop("b")).astype(BF16)
            part = _dot(hidden, wout_ref[cols, :])
            tmp["y"] = part if "y" not in tmp else tmp["y"] + part
        return run

    items = [mix_item]
    start = 0
    for width in FF_GROUPS:
        cols = slice(start, start + width)
        gate_cols = slice(D_FF + start, D_FF + start + width)
        items += [ffn_in_item("a", cols), ffn_in_item("b", gate_cols), ffn_out_item(cols)]
        start += width
    fill, drain = _make_fill(items, -(-len(items) // (3 * nch)))

    tmp["y_gla"] = _dot(a_ref[...], wgo_ref[...])

    for ci in range(nch):
        rows = slice(ci * CHUNK, (ci + 1) * CHUNK)
        b_mid = fst_ref[0, 0, ci:ci + 1, 0:D_KEY]
        b_last = fst_ref[0, 0, ci:ci + 1, D_KEY:]
        o = _gla_chunk(qf_ref[0, rows, :], kf_ref[0, rows, :], v_ref[0, rows, :], st_ref,
                       jnp.exp(b_mid), jnp.exp(b_last), jnp.exp(b_last - b_mid), False, fill)
        o = o + ob_ref[0, rows, :]
        for h in range(N_HEADS):
            vs = slice(h * HEAD_V, (h + 1) * HEAD_V)
            a_ref[rows, vs] = (_rms(o[:, vs]) * gh_ref[...]
                               * gs_ref[0, rows, vs].astype(F32)).astype(BF16)
    drain()

    x2 = tmp["x1"] + mod_ref[0, 5:6, :] * tmp["y"]
    o_ref[0] = _rms(x2) * gfin_ref[...]


def _fwd_ffn_call(x, mod, qf, kf, v, ob, gs, sga, mb, fst, g_head, w_go, w_mo, s0, g_ffn, w_in,
                  w_out, g_final):
    bsz, t, _ = x.shape
    tm = TM_FWD
    nt = t // tm
    nch = tm // CHUNK
    steps = bsz * nt
    fst = fst.reshape(bsz, nt, nch, 2 * D_KEY)
    scan = lambda s: (jnp.minimum(s, steps - 1) // nt, jnp.minimum(s, steps - 1) % nt, 0)
    rest = lambda s: (jnp.maximum(s - 1, 0) // nt, jnp.maximum(s - 1, 0) % nt, 0)
    const = lambda s: (0, 0)
    scan_tok = lambda width: pl.BlockSpec((1, tm, width), scan)
    rest_tok = lambda width: pl.BlockSpec((1, tm, width), rest)
    return pl.pallas_call(
        functools.partial(_fwd_ffn_kernel, nt, steps),
        grid=(steps + 1,),
        in_specs=[
            scan_tok(D_KEY), scan_tok(D_KEY), scan_tok(D_VAL), scan_tok(D_VAL), scan_tok(D_VAL),
            pl.BlockSpec((1, 1, nch, 2 * D_KEY), lambda s: scan(s) + (0,)),
            pl.BlockSpec((1, HEAD_V, D_KEY), lambda s: (jnp.minimum(s, steps - 1) // nt, 0, 0)),
            pl.BlockSpec((1, HEAD_V), const),
            rest_tok(D_MODEL), rest_tok(D_MODEL), rest_tok(D_MODEL),
            pl.BlockSpec((1, 6, D_MODEL), lambda s: (jnp.maximum(s - 1, 0) // nt, 0, 0)),
            pl.BlockSpec(w_go.shape, const),
            pl.BlockSpec(w_mo.shape, const),
            pl.BlockSpec((1, D_MODEL), const),
            pl.BlockSpec(w_in.shape, const),
            pl.BlockSpec(w_out.shape, const),
            pl.BlockSpec((1, D_MODEL), const),
        ],
        out_specs=rest_tok(D_MODEL),
        out_shape=jax.ShapeDtypeStruct((bsz, t, D_MODEL), F32),
        scratch_shapes=[pltpu.VMEM((HEAD_V, D_KEY), F32), pltpu.VMEM((tm, D_VAL), BF16)],
        compiler_params=pltpu.CompilerParams(
            dimension_semantics=("arbitrary",), vmem_limit_bytes=VMEM_LIMIT),
        name="fwd_ffn",
    )(qf, kf, v, ob, gs, fst, s0, g_head, x, sga, mb, mod, w_go, w_mo, g_ffn, w_in, w_out, g_final)


def kernel(x, c, ctx, c_ctx, w_ada, b_ada, g_mix, w_in, w_dec_up, b_dec, g_head, w_gla_out,
           w_conv, w_conv_out, w_mix_out, g_ffn, w_ffn_in, w_ffn_out, g_final):
    assert w_ada.shape[0] == 1, "single-layer block"
    bsz = x.shape[0]
    assert x.shape[1] % TM_REV == 0 and x.shape[1] % TM_FWD == 0
    assert ctx.shape[1] % CHUNK == 0 and sum(FF_GROUPS) == D_FF

    o_g = 2 * D_KEY + 2 * D_VAL
    o_c = o_g + 2 * DECAY_RANK
    o_gate = o_c + 3 * D_MODEL
    piece = lambda lo, hi: w_in[0, :, lo:hi].astype(BF16)
    w_qkvg, w_dd, w_c3, w_gate = piece(0, o_g), piece(o_g, o_c), piece(o_c, o_gate), piece(o_gate, None)
    zeros = jnp.zeros((DECAY_RANK, D_KEY), F32)
    w_dec = jnp.concatenate([jnp.concatenate([w_dec_up[0, 0], zeros], axis=1),
                             jnp.concatenate([zeros, w_dec_up[0, 1]], axis=1)], axis=0).astype(BF16)
    b_dec2 = b_dec[0].reshape(1, 2 * D_KEY)

    n_rows = -(-(bsz + 1) // 8) * 8
    c_rows = jnp.zeros((n_rows, D_MODEL), F32).at[:bsz].set(c).at[bsz].set(c_ctx)
    mod = _ada_call(c_rows, w_ada[0], b_ada).reshape(n_rows, 6, D_MODEL)

    g_mix2 = g_mix.reshape(1, D_MODEL)
    s_cf, s_cb = _ctx_call(ctx, mod, g_mix2, piece(D_KEY, 2 * D_KEY), piece(2 * D_KEY, 2 * D_KEY + D_VAL),
                           w_dd, w_dec, b_dec2, bsz)
    qf, kf, v, ob, gs, sga, mb, fst = _proj_bwd_call(
        x, mod, g_mix2, w_qkvg, w_dd, w_dec, b_dec2, w_c3, w_gate, w_conv[0],
        w_conv_out[0].astype(BF16), s_cb)
    return _fwd_ffn_call(x, mod, qf, kf, v, ob, gs, sga, mb, fst, g_head.reshape(1, HEAD_V),
                         w_gla_out[0].astype(BF16), w_mix_out[0].astype(BF16), s_cf,
                         g_ffn.reshape(1, D_MODEL), w_ffn_in[0].astype(BF16),
                         w_ffn_out[0].astype(BF16), g_final.reshape(1, D_MODEL))
```

```python
import functools

import jax
import jax.numpy as jnp
from jax import lax
from jax.experimental import pallas as pl
from jax.experimental.pallas import tpu as pltpu

D_MODEL = 1024
N_HEADS = 4
D_KEY = 512
D_VAL = 1024
HEAD_K = D_KEY // N_HEADS
HEAD_V = D_VAL // N_HEADS
DECAY_RANK = 16
DECAY_TAU = 16.0
CHUNK = 64
GRID_W = 64
D_FF = 2816
EPS = 1e-6

MXU_TILE = 256
TM_REV = 512
TM_FWD = 512
PROJ_PARTS = 4
COL_QK, COL_V, COL_G, COL_DD = 0, 2 * D_KEY, 2 * D_KEY + D_VAL, 2 * D_KEY + 2 * D_VAL
DD_PAD = 128
COL_CONV = COL_DD + D_MODEL
COL_GATE = COL_CONV + 3 * D_MODEL
FF_GROUPS = (512, 512, 512, 512, 512, 256)
VMEM_LIMIT = 60 * 1024 * 1024

BF16 = jnp.bfloat16
F32 = jnp.float32

NT_DIMS = (((1,), (1,)), ((), ()))
TN_DIMS = (((0,), (0,)), ((), ()))


def _dot(a, b):
    return jnp.dot(a, b, preferred_element_type=F32)


def _sigmoid(x):
    return 1.0 / (1.0 + jnp.exp(-x))


def _silu(x):
    return x * _sigmoid(x)


def _log_sigmoid(x):
    return jnp.minimum(x, 0.0) - jnp.log(1.0 + jnp.exp(-jnp.abs(x)))


def _rms(x):
    return x * lax.rsqrt(jnp.mean(x * x, axis=-1, keepdims=True) + EPS)


def _norm_modulate(x, g, shift, scale):
    return (_rms(x) * g * (1.0 + scale) + shift).astype(BF16)


def _cumsum_matrix(n, upper):
    r = lax.broadcasted_iota(jnp.int32, (n, n), 0)
    c = lax.broadcasted_iota(jnp.int32, (n, n), 1)
    same = (r // CHUNK) == (c // CHUNK)
    tri = (c >= r) if upper else (c <= r)
    return jnp.where(same & tri, 1.0, 0.0).astype(BF16)


def _chunk_cumsum(la, upper):
    rows = min(la.shape[0], MXU_TILE)
    mat = _cumsum_matrix(rows, upper)
    hi = la.astype(BF16)
    lo = (la - hi.astype(F32)).astype(BF16)
    parts = [_dot(mat, hi[r:r + rows]) + _dot(mat, lo[r:r + rows])
             for r in range(0, la.shape[0], rows)]
    return parts[0] if len(parts) == 1 else jnp.concatenate(parts, axis=0)


def _log_decay(hb, wdd_ref, wdec_ref, bdec_ref):
    dd = _dot(hb, wdd_ref[...]).astype(BF16)
    zd = _dot(dd, wdec_ref[...]) + bdec_ref[...]
    return _log_sigmoid(zd) * (1.0 / DECAY_TAU)


def _kv_all_heads(kc, vc):
    v_stack = jnp.concatenate([vc[:, h * HEAD_V:(h + 1) * HEAD_V] for h in range(N_HEADS)], axis=0)
    r = lax.broadcasted_iota(jnp.int32, (N_HEADS * CHUNK, D_KEY), 0) // CHUNK
    c = lax.broadcasted_iota(jnp.int32, (N_HEADS * CHUNK, D_KEY), 1) // HEAD_K
    k_blk = jnp.where(r == c, jnp.concatenate([kc] * N_HEADS, axis=0), jnp.zeros((), kc.dtype))
    return lax.dot_general(v_stack, k_blk, TN_DIMS, preferred_element_type=F32)


def _make_fill(items, per_gap):
    queue = list(reversed(items))

    def fill(n=per_gap):
        for _ in range(n):
            if queue:
                queue.pop()()

    def drain():
        while queue:
            queue.pop()()

    return fill, drain


def _no_fill():
    pass


def _gla_chunk(qc, kc, vc, st_ref, e_mid, e_last, e_gap, upper, fill=_no_fill):
    r = lax.broadcasted_iota(jnp.int32, (CHUNK, CHUNK), 0)
    c = lax.broadcasted_iota(jnp.int32, (CHUNK, CHUNK), 1)
    keep = (c >= r) if upper else (c <= r)
    heads = [(slice(h * HEAD_K, (h + 1) * HEAD_K), slice(h * HEAD_V, (h + 1) * HEAD_V))
             for h in range(N_HEADS)]
    scores = [lax.dot_general(qc[:, ks], kc[:, ks], NT_DIMS, preferred_element_type=F32)
              for ks, _ in heads]
    fill()
    st = st_ref[...]
    st_in = (st * e_mid).astype(BF16)
    outs = []
    for (ks, vs), s in zip(heads, scores):
        a = jnp.where(keep, s, 0.0).astype(BF16)
        outs.append(_dot(a, vc[:, vs])
                    + lax.dot_general(qc[:, ks], st_in[:, ks], NT_DIMS, preferred_element_type=F32))
    fill()
    st_ref[...] = st * e_last + _kv_all_heads(kc, vc) * e_gap
    fill()
    return jnp.concatenate(outs, axis=-1)


def _state_update_only(kc, vc, st_ref, e_last, e_gap):
    st_ref[...] = st_ref[...] * e_last + _kv_all_heads(kc, vc) * e_gap


def _ada_kernel(c_ref, w_ref, b_ref, o_ref):
    s = _silu(c_ref[...]).astype(BF16)
    o_ref[...] = _dot(s, w_ref[...].astype(BF16)) + b_ref[...]


def _ada_call(c_rows, w_ada, b_ada):
    n_rows = c_rows.shape[0]
    n_out = w_ada.shape[1]
    bn = 512
    return pl.pallas_call(
        _ada_kernel,
        grid=(n_out // bn,),
        in_specs=[
            pl.BlockSpec((n_rows, D_MODEL), lambda j: (0, 0)),
            pl.BlockSpec((D_MODEL, bn), lambda j: (0, j)),
            pl.BlockSpec((1, bn), lambda j: (0, j)),
        ],
        out_specs=pl.BlockSpec((n_rows, bn), lambda j: (0, j)),
        out_shape=jax.ShapeDtypeStruct((n_rows, n_out), F32),
        name="ada",
    )(c_rows, w_ada, b_ada)


def _ctx_kernel(x_ref, mod_ref, g_ref, wk_ref, wv_ref, wdd_ref, wdec_ref, bdec_ref,
                sf_ref, sb_ref):
    n = x_ref.shape[1]
    nch = n // CHUNK
    hb = _norm_modulate(x_ref[0], g_ref[...], mod_ref[0, 0:1, :], mod_ref[0, 1:2, :])
    la = _log_decay(hb, wdd_ref, wdec_ref, bdec_ref)
    b_f = _chunk_cumsum(la[:, :D_KEY], False)
    b_b = _chunk_cumsum(la[:, D_KEY:], True)
    k = _dot(hb, wk_ref[...])
    v = _dot(hb, wv_ref[...]).astype(BF16)
    sf_ref[0] = jnp.zeros(sf_ref.shape[1:], F32)
    sb_ref[0] = jnp.zeros(sb_ref.shape[1:], F32)
    mid = CHUNK // 2
    for ci in range(nch):
        rows = slice(ci * CHUNK, (ci + 1) * CHUNK)
        bc = b_f[rows]
        b_mid, b_last = bc[mid - 1:mid], bc[CHUNK - 1:CHUNK]
        kc = (k[rows] * jnp.exp(b_mid - bc)).astype(BF16)
        _state_update_only(kc, v[rows], sf_ref.at[0], jnp.exp(b_last), jnp.exp(b_last - b_mid))
        rows = slice((nch - 1 - ci) * CHUNK, (nch - ci) * CHUNK)
        bc = b_b[rows]
        b_mid, b_last = bc[mid:mid + 1], bc[0:1]
        kc = (k[rows] * jnp.exp(b_mid - bc)).astype(BF16)
        _state_update_only(kc, v[rows], sb_ref.at[0], jnp.exp(b_last), jnp.exp(b_last - b_mid))


def _ctx_call(ctx, mod, g_mix, w_pad, w_dec, b_dec, ctx_row):
    bsz, n, _ = ctx.shape
    const = lambda b: (0, 0)
    w_cols = lambda width, start: pl.BlockSpec((D_MODEL, width), lambda b: (0, start // width),
                                               pipeline_mode=pl.Buffered(1))
    st_spec = pl.BlockSpec((1, HEAD_V, D_KEY), lambda b: (b, 0, 0))
    st_shape = jax.ShapeDtypeStruct((bsz, HEAD_V, D_KEY), F32)
    return pl.pallas_call(
        _ctx_kernel,
        grid=(bsz,),
        in_specs=[
            pl.BlockSpec((1, n, D_MODEL), lambda b: (b, 0, 0)),
            pl.BlockSpec((1, 6, D_MODEL), lambda b: (ctx_row, 0, 0)),
            pl.BlockSpec((1, D_MODEL), const),
            w_cols(D_KEY, COL_QK + D_KEY),
            w_cols(D_VAL, COL_V),
            w_cols(DD_PAD, COL_DD),
            pl.BlockSpec(w_dec.shape, const),
            pl.BlockSpec(b_dec.shape, const),
        ],
        out_specs=[st_spec, st_spec],
        out_shape=[st_shape, st_shape],
        compiler_params=pltpu.CompilerParams(vmem_limit_bytes=VMEM_LIMIT),
        name="ctx_states",
    )(ctx, mod, g_mix, w_pad, w_pad, w_pad, w_dec, b_dec)


def _proj_bwd_kernel(nt, x0_ref, mod0_ref, xn_ref, modn_ref, g_ref, wqk_ref, wv_ref, wg_ref, wdd_ref,
                     wdec_ref, bdec_ref, wcb_ref, wcc_ref, wch_ref, wga_ref, wgb_ref, wconv_ref,
                     wco_ref, s0_ref,
                     qf_ref, kf_ref, v_ref, ob_ref, gs_ref, sga_ref, mb_ref, fst_ref,
                     st_ref, p_ref, hb_ref):
    tm = xn_ref.shape[1]
    nch = tm // CHUNK
    mid = CHUNK // 2
    s = pl.program_id(0)
    cur = s % 2

    @pl.when(s == 0)
    def _():
        hb_ref[0] = _norm_modulate(x0_ref[0], g_ref[...], mod0_ref[0, 0:1, :], mod0_ref[0, 1:2, :])

    @pl.when(s % nt == 0)
    def _():
        st_ref[...] = s0_ref[0]

    hb = hb_ref.at[cur]
    width = D_MODEL // PROJ_PARTS
    col_groups = tuple(slice(j * width, (j + 1) * width) for j in range(PROJ_PARTS))
    pos = lax.broadcasted_iota(jnp.int32, (tm, 1), 0) % GRID_W
    tmp = {}

    def gate_item(cols):
        def run():
            g = _dot(hb[...], wg_ref[:, cols])
            gs_ref[0, :, cols] = _silu(g).astype(BF16)
        return run

    def conv_in_item(name, w_ref, cols):
        def run():
            tmp[name] = _dot(hb[...], w_ref[:, cols])
        return run

    def conv_item(cols):
        def run():
            cb = _dot(hb[...], wcb_ref[:, cols])
            u = tmp.pop("cc") * tmp.pop("ch")
            u_prev = jnp.where(pos == 0, 0.0, pltpu.roll(u, 1, 0))
            u_next = jnp.where(pos == GRID_W - 1, 0.0, pltpu.roll(u, tm - 1, 0))
            uc = (u_prev * wconv_ref[0:1, cols] + u * wconv_ref[1:2, cols]
                  + u_next * wconv_ref[2:3, cols])
            p_ref[:, cols] = (cb * uc).astype(BF16)
        return run

    def conv_out_item(cols):
        def run():
            tmp["yc"] = _dot(p_ref[...], wco_ref[:, cols])
        return run

    def merge_gate_item(cols):
        def run():
            gb = _dot(hb[...], wgb_ref[:, cols])
            mb_ref[0, :, cols] = (_sigmoid(gb) * tmp.pop("yc")).astype(BF16)
        return run

    def gla_gate_item(cols):
        def run():
            ga = _dot(hb[...], wga_ref[:, cols])
            sga_ref[0, :, cols] = _sigmoid(ga).astype(BF16)
        return run

    def next_input_item():
        hb_ref[1 - cur] = _norm_modulate(xn_ref[0], g_ref[...], modn_ref[0, 0:1, :],
                                         modn_ref[0, 1:2, :])

    items = [gate_item(c) for c in col_groups]
    for c in col_groups:
        items += [conv_in_item("cc", wcc_ref, c), conv_in_item("ch", wch_ref, c), conv_item(c)]
    items += [next_input_item]
    for c in col_groups:
        items += [conv_out_item(c), merge_gate_item(c)]
    items += [gla_gate_item(c) for c in col_groups]
    fill, drain = _make_fill(items, 1)

    la = _log_decay(hb[...], wdd_ref, wdec_ref, bdec_ref)
    fill()
    b_f = _chunk_cumsum(la[:, :D_KEY], False)
    b_b = _chunk_cumsum(la[:, D_KEY:], True)
    fill()

    q = _dot(hb[...], wqk_ref[:, 0:D_KEY]) * (HEAD_K ** -0.5)
    k = _dot(hb[...], wqk_ref[:, D_KEY:])
    vb = _dot(hb[...], wv_ref[...]).astype(BF16)
    v_ref[0] = vb

    for ci in reversed(range(nch)):
        rows = slice(ci * CHUNK, (ci + 1) * CHUNK)
        bc = b_f[rows]
        b_mid, b_last = bc[mid - 1:mid], bc[CHUNK - 1:CHUNK]
        d = bc - b_mid
        qf_ref[0, rows, :] = (q[rows] * jnp.exp(d)).astype(BF16)
        kf_ref[0, rows, :] = (k[rows] * jnp.exp(-d)).astype(BF16)
        fst_ref[0, 0, ci:ci + 1, :] = jnp.concatenate([b_mid, b_last], axis=-1)
        bc = b_b[rows]
        b_mid, b_last = bc[mid:mid + 1], bc[0:1]
        d = bc - b_mid
        qc = (q[rows] * jnp.exp(d)).astype(BF16)
        kc = (k[rows] * jnp.exp(-d)).astype(BF16)
        ob_ref[0, rows, :] = _gla_chunk(qc, kc, vb[rows], st_ref, jnp.exp(b_mid),
                                        jnp.exp(b_last), jnp.exp(b_last - b_mid), True, fill)
    drain()


def _proj_bwd_call(x, mod, g_mix, w_pad, w_dec, b_dec, w_conv, w_co, s0):
    bsz, t, _ = x.shape
    tm = TM_REV
    nt = t // tm
    nch = tm // CHUNK
    steps = bsz * nt
    cur = lambda s: (s // nt, nt - 1 - s % nt, 0)
    nxt = lambda s: cur(jnp.minimum(s + 1, steps - 1))
    const = lambda s: (0, 0)
    w_cols = lambda width, start: pl.BlockSpec((D_MODEL, width), lambda s: (0, start // width),
                                               pipeline_mode=pl.Buffered(1))
    tok = lambda width: pl.BlockSpec((1, tm, width), cur)
    tok_shape = lambda width, dt: jax.ShapeDtypeStruct((bsz, t, width), dt)
    return pl.pallas_call(
        functools.partial(_proj_bwd_kernel, nt),
        grid=(steps,),
        in_specs=[
            pl.BlockSpec((1, tm, D_MODEL), lambda s: (0, nt - 1, 0)),
            pl.BlockSpec((1, 6, D_MODEL), lambda s: (0, 0, 0)),
            pl.BlockSpec((1, tm, D_MODEL), nxt),
            pl.BlockSpec((1, 6, D_MODEL), lambda s: (jnp.minimum(s + 1, steps - 1) // nt, 0, 0)),
            pl.BlockSpec((1, D_MODEL), const),
            w_cols(2 * D_KEY, COL_QK), w_cols(D_VAL, COL_V), w_cols(D_VAL, COL_G),
            w_cols(DD_PAD, COL_DD),
            pl.BlockSpec(w_dec.shape, const),
            pl.BlockSpec(b_dec.shape, const),
            w_cols(D_MODEL, COL_CONV), w_cols(D_MODEL, COL_CONV + D_MODEL),
            w_cols(D_MODEL, COL_CONV + 2 * D_MODEL),
            w_cols(D_MODEL, COL_GATE), w_cols(D_MODEL, COL_GATE + D_MODEL),
            pl.BlockSpec(w_conv.shape, const),
            pl.BlockSpec(w_co.shape, const),
            pl.BlockSpec((1, HEAD_V, D_KEY), lambda s: (s // nt, 0, 0)),
        ],
        out_specs=[
            tok(D_KEY), tok(D_KEY), tok(D_VAL), tok(D_VAL), tok(D_VAL), tok(D_MODEL), tok(D_MODEL),
            pl.BlockSpec((1, 1, nch, 2 * D_KEY), lambda s: (s // nt, nt - 1 - s % nt, 0, 0)),
        ],
        out_shape=[
            tok_shape(D_KEY, BF16), tok_shape(D_KEY, BF16), tok_shape(D_VAL, BF16),
            tok_shape(D_VAL, F32), tok_shape(D_VAL, BF16), tok_shape(D_MODEL, BF16),
            tok_shape(D_MODEL, BF16),
            jax.ShapeDtypeStruct((bsz, nt, nch, 2 * D_KEY), F32),
        ],
        scratch_shapes=[pltpu.VMEM((HEAD_V, D_KEY), F32), pltpu.VMEM((tm, D_MODEL), BF16),
                        pltpu.VMEM((2, tm, D_MODEL), BF16)],
        compiler_params=pltpu.CompilerParams(
            dimension_semantics=("arbitrary",), vmem_limit_bytes=VMEM_LIMIT),
        name="proj_bwd",
    )(x, mod, x, mod, g_mix, w_pad, w_pad, w_pad, w_pad, w_dec, b_dec, w_pad, w_pad, w_pad, w_pad, w_pad,
      w_conv, w_co, s0)


def _fwd_ffn_kernel(nt, steps, qf_ref, kf_ref, v_ref, ob_ref, gs_ref, fst_ref, s0_ref, gh_ref,
                    x_ref, sga_ref, mb_ref, mod_ref, wgo_ref, wmo_ref, gffn_ref, win_ref, wout_ref,
                    gfin_ref, o_ref, st_ref, a_ref):
    tm = x_ref.shape[1]
    nch = tm // CHUNK
    s = pl.program_id(0)
    t = jnp.minimum(s, steps - 1)

    @pl.when(s == 0)
    def _():
        a_ref[...] = jnp.zeros(a_ref.shape, a_ref.dtype)

    @pl.when(t % nt == 0)
    def _():
        st_ref[...] = s0_ref[0]

    tmp = {}

    def mix_item():
        merged = sga_ref[0].astype(F32) * tmp.pop("y_gla") + mb_ref[0].astype(F32)
        mix = _dot(merged.astype(BF16), wmo_ref[...])
        x1 = x_ref[0] + mod_ref[0, 2:3, :] * mix
        tmp["x1"] = x1
        tmp["hb"] = _norm_modulate(x1, gffn_ref[...], mod_ref[0, 3:4, :], mod_ref[0, 4:5, :])

    def ffn_in_item(name, cols):
        def run():
            tmp[name] = _dot(tmp["hb"], win_ref[:, cols])
        return run

    def ffn_out_item(cols):
        def run():
            hidden = (_silu(tmp.pop("a")) * tmp.pop("b")).astype(BF16)
            part = _dot(hidden, wout_ref[cols, :])
            tmp["y"] = part if "y" not in tmp else tmp["y"] + part
        return run

    items = [mix_item]
    start = 0
    for width in FF_GROUPS:
        cols = slice(start, start + width)
        gate_cols = slice(D_FF + start, D_FF + start + width)
        items += [ffn_in_item("a", cols), ffn_in_item("b", gate_cols), ffn_out_item(cols)]
        start += width
    fill, drain = _make_fill(items, -(-len(items) // (3 * nch)))

    tmp["y_gla"] = _dot(a_ref[...], wgo_ref[...])

    for ci in range(nch):
        rows = slice(ci * CHUNK, (ci + 1) * CHUNK)
        b_mid = fst_ref[0, 0, ci:ci + 1, 0:D_KEY]
        b_last = fst_ref[0, 0, ci:ci + 1, D_KEY:]
        o = _gla_chunk(qf_ref[0, rows, :], kf_ref[0, rows, :], v_ref[0, rows, :], st_ref,
                       jnp.exp(b_mid), jnp.exp(b_last), jnp.exp(b_last - b_mid), False, fill)
        o = o + ob_ref[0, rows, :]
        for h in range(N_HEADS):
            vs = slice(h * HEAD_V, (h + 1) * HEAD_V)
            a_ref[rows, vs] = (_rms(o[:, vs]) * gh_ref[...]
                               * gs_ref[0, rows, vs].astype(F32)).astype(BF16)
    drain()

    x2 = tmp["x1"] + mod_ref[0, 5:6, :] * tmp["y"]
    o_ref[0] = _rms(x2) * gfin_ref[...]


def _fwd_ffn_call(x, mod, qf, kf, v, ob, gs, sga, mb, fst, g_head, w_go, w_mo, s0, g_ffn, w_in,
                  w_out, g_final):
    bsz, t, _ = x.shape
    tm = TM_FWD
    nt = t // tm
    nch = tm // CHUNK
    steps = bsz * nt
    fst = fst.reshape(bsz, nt, nch, 2 * D_KEY)
    scan = lambda s: (jnp.minimum(s, steps - 1) // nt, jnp.minimum(s, steps - 1) % nt, 0)
    rest = lambda s: (jnp.maximum(s - 1, 0) // nt, jnp.maximum(s - 1, 0) % nt, 0)
    const = lambda s: (0, 0)
    scan_tok = lambda width: pl.BlockSpec((1, tm, width), scan)
    rest_tok = lambda width: pl.BlockSpec((1, tm, width), rest)
    return pl.pallas_call(
        functools.partial(_fwd_ffn_kernel, nt, steps),
        grid=(steps + 1,),
        in_specs=[
            scan_tok(D_KEY), scan_tok(D_KEY), scan_tok(D_VAL), scan_tok(D_VAL), scan_tok(D_VAL),
            pl.BlockSpec((1, 1, nch, 2 * D_KEY), lambda s: scan(s) + (0,)),
            pl.BlockSpec((1, HEAD_V, D_KEY), lambda s: (jnp.minimum(s, steps - 1) // nt, 0, 0)),
            pl.BlockSpec((1, HEAD_V), const),
            rest_tok(D_MODEL), rest_tok(D_MODEL), rest_tok(D_MODEL),
            pl.BlockSpec((1, 6, D_MODEL), lambda s: (jnp.maximum(s - 1, 0) // nt, 0, 0)),
            pl.BlockSpec(w_go.shape, const),
            pl.BlockSpec(w_mo.shape, const),
            pl.BlockSpec((1, D_MODEL), const),
            pl.BlockSpec(w_in.shape, const),
            pl.BlockSpec(w_out.shape, const),
            pl.BlockSpec((1, D_MODEL), const),
        ],
        out_specs=rest_tok(D_MODEL),
        out_shape=jax.ShapeDtypeStruct((bsz, t, D_MODEL), F32),
        scratch_shapes=[pltpu.VMEM((HEAD_V, D_KEY), F32), pltpu.VMEM((tm, D_VAL), BF16)],
        compiler_params=pltpu.CompilerParams(
            dimension_semantics=("arbitrary",), vmem_limit_bytes=VMEM_LIMIT),
        name="fwd_ffn",
    )(qf, kf, v, ob, gs, fst, s0, g_head, x, sga, mb, mod, w_go, w_mo, g_ffn, w_in, w_out, g_final)


def kernel(x, c, ctx, c_ctx, w_ada, b_ada, g_mix, w_in, w_dec_up, b_dec, g_head, w_gla_out,
           w_conv, w_conv_out, w_mix_out, g_ffn, w_ffn_in, w_ffn_out, g_final):
    assert w_ada.shape[0] == 1, "single-layer block"
    bsz = x.shape[0]
    assert x.shape[1] % TM_REV == 0 and x.shape[1] % TM_FWD == 0
    assert ctx.shape[1] % CHUNK == 0 and sum(FF_GROUPS) == D_FF

    o_dd = COL_DD + 2 * DECAY_RANK
    w_pad = jnp.concatenate([w_in[0, :, :o_dd], jnp.zeros((D_MODEL, COL_CONV - o_dd), F32),
                             w_in[0, :, o_dd:]], axis=1).astype(BF16)
    w_dec = jnp.zeros((DD_PAD, 2 * D_KEY), F32)
    w_dec = w_dec.at[:DECAY_RANK, :D_KEY].set(w_dec_up[0, 0])
    w_dec = w_dec.at[DECAY_RANK:2 * DECAY_RANK, D_KEY:].set(w_dec_up[0, 1]).astype(BF16)
    b_dec2 = b_dec[0].reshape(1, 2 * D_KEY)

    n_rows = -(-(bsz + 1) // 8) * 8
    c_rows = jnp.zeros((n_rows, D_MODEL), F32).at[:bsz].set(c).at[bsz].set(c_ctx)
    mod = _ada_call(c_rows, w_ada[0], b_ada).reshape(n_rows, 6, D_MODEL)

    g_mix2 = g_mix.reshape(1, D_MODEL)
    s_cf, s_cb = _ctx_call(ctx, mod, g_mix2, w_pad, w_dec, b_dec2, bsz)
    qf, kf, v, ob, gs, sga, mb, fst = _proj_bwd_call(
        x, mod, g_mix2, w_pad, w_dec, b_dec2, w_conv[0], w_conv_out[0].astype(BF16), s_cb)
    return _fwd_ffn_call(x, mod, qf, kf, v, ob, gs, sga, mb, fst, g_head.reshape(1, HEAD_V),
                         w_gla_out[0].astype(BF16), w_mix_out[0].astype(BF16), s_cf,
                         g_ffn.reshape(1, D_MODEL), w_ffn_in[0].astype(BF16),
                         w_ffn_out[0].astype(BF16), g_final.reshape(1, D_MODEL))
```

```python
import functools

import jax
import jax.numpy as jnp
from jax import lax
from jax.experimental import pallas as pl
from jax.experimental.pallas import tpu as pltpu

D_MODEL = 1024
N_HEADS = 4
D_KEY = 512
D_VAL = 1024
HEAD_K = D_KEY // N_HEADS
HEAD_V = D_VAL // N_HEADS
DECAY_RANK = 16
DECAY_TAU = 16.0
CHUNK = 64
GRID_W = 64
D_FF = 2816
EPS = 1e-6

MXU_TILE = 256
TM_REV = 512
TM_FWD = 512
PROJ_PARTS = 4
ROW_Q, ROW_K, ROW_V, ROW_G = 0, D_KEY, 2 * D_KEY, 2 * D_KEY + D_VAL
ROW_DD = ROW_G + D_VAL
ROW_CB = ROW_DD + 2 * DECAY_RANK
ROW_CC, ROW_CH, ROW_GA, ROW_GB = (ROW_CB + D_MODEL, ROW_CB + 2 * D_MODEL, ROW_CB + 3 * D_MODEL,
                                  ROW_CB + 4 * D_MODEL)
N_IN = ROW_GB + D_MODEL
FF_GROUPS = (512, 512, 512, 512, 512, 256)
VMEM_LIMIT = 60 * 1024 * 1024

BF16 = jnp.bfloat16
F32 = jnp.float32

NT_DIMS = (((1,), (1,)), ((), ()))
TN_DIMS = (((0,), (0,)), ((), ()))


def _dot(a, b):
    return jnp.dot(a, b, preferred_element_type=F32)


def _dot_t(a, b_t):
    return lax.dot_general(a, b_t, NT_DIMS, preferred_element_type=F32)


def _sigmoid(x):
    return 1.0 / (1.0 + jnp.exp(-x))


def _silu(x):
    return x * _sigmoid(x)


def _log_sigmoid(x):
    return jnp.minimum(x, 0.0) - jnp.log(1.0 + jnp.exp(-jnp.abs(x)))


def _rms(x):
    return x * lax.rsqrt(jnp.mean(x * x, axis=-1, keepdims=True) + EPS)


def _norm_modulate(x, g, shift, scale):
    return (_rms(x) * g * (1.0 + scale) + shift).astype(BF16)


def _cumsum_matrix(n, upper):
    r = lax.broadcasted_iota(jnp.int32, (n, n), 0)
    c = lax.broadcasted_iota(jnp.int32, (n, n), 1)
    same = (r // CHUNK) == (c // CHUNK)
    tri = (c >= r) if upper else (c <= r)
    return jnp.where(same & tri, 1.0, 0.0).astype(BF16)


def _chunk_cumsum(la, upper):
    rows = min(la.shape[0], MXU_TILE)
    mat = _cumsum_matrix(rows, upper)
    hi = la.astype(BF16)
    lo = (la - hi.astype(F32)).astype(BF16)
    parts = [_dot(mat, hi[r:r + rows]) + _dot(mat, lo[r:r + rows])
             for r in range(0, la.shape[0], rows)]
    return parts[0] if len(parts) == 1 else jnp.concatenate(parts, axis=0)


def _log_decay(hb, wdd_ref, wdec_ref, bdec_ref):
    dd = _dot_t(hb, wdd_ref[...]).astype(BF16)
    zd = _dot(dd, wdec_ref[...]) + bdec_ref[...]
    return _log_sigmoid(zd) * (1.0 / DECAY_TAU)


def _kv_all_heads(kc, vc):
    v_stack = jnp.concatenate([vc[:, h * HEAD_V:(h + 1) * HEAD_V] for h in range(N_HEADS)], axis=0)
    r = lax.broadcasted_iota(jnp.int32, (N_HEADS * CHUNK, D_KEY), 0) // CHUNK
    c = lax.broadcasted_iota(jnp.int32, (N_HEADS * CHUNK, D_KEY), 1) // HEAD_K
    k_blk = jnp.where(r == c, jnp.concatenate([kc] * N_HEADS, axis=0), jnp.zeros((), kc.dtype))
    return lax.dot_general(v_stack, k_blk, TN_DIMS, preferred_element_type=F32)


def _make_fill(items, per_gap):
    queue = list(reversed(items))

    def fill(n=per_gap):
        for _ in range(n):
            if queue:
                queue.pop()()

    def drain():
        while queue:
            queue.pop()()

    return fill, drain


def _no_fill():
    pass


def _gla_chunk(qc, kc, vc, st_ref, e_mid, e_last, e_gap, upper, fill=_no_fill):
    r = lax.broadcasted_iota(jnp.int32, (CHUNK, CHUNK), 0)
    c = lax.broadcasted_iota(jnp.int32, (CHUNK, CHUNK), 1)
    keep = (c >= r) if upper else (c <= r)
    heads = [(slice(h * HEAD_K, (h + 1) * HEAD_K), slice(h * HEAD_V, (h + 1) * HEAD_V))
             for h in range(N_HEADS)]
    scores = [lax.dot_general(qc[:, ks], kc[:, ks], NT_DIMS, preferred_element_type=F32)
              for ks, _ in heads]
    fill()
    st = st_ref[...]
    st_in = (st * e_mid).astype(BF16)
    outs = []
    for (ks, vs), s in zip(heads, scores):
        a = jnp.where(keep, s, 0.0).astype(BF16)
        outs.append(_dot(a, vc[:, vs])
                    + lax.dot_general(qc[:, ks], st_in[:, ks], NT_DIMS, preferred_element_type=F32))
    fill()
    st_ref[...] = st * e_last + _kv_all_heads(kc, vc) * e_gap
    fill()
    return jnp.concatenate(outs, axis=-1)


def _state_update_only(kc, vc, st_ref, e_last, e_gap):
    st_ref[...] = st_ref[...] * e_last + _kv_all_heads(kc, vc) * e_gap


def _ada_kernel(c_ref, w_ref, b_ref, o_ref):
    s = _silu(c_ref[...]).astype(BF16)
    o_ref[...] = _dot(s, w_ref[...].astype(BF16)) + b_ref[...]


def _ada_call(c_rows, w_ada, b_ada):
    n_rows = c_rows.shape[0]
    n_out = w_ada.shape[1]
    bn = 512
    return pl.pallas_call(
        _ada_kernel,
        grid=(n_out // bn,),
        in_specs=[
            pl.BlockSpec((n_rows, D_MODEL), lambda j: (0, 0)),
            pl.BlockSpec((D_MODEL, bn), lambda j: (0, j)),
            pl.BlockSpec((1, bn), lambda j: (0, j)),
        ],
        out_specs=pl.BlockSpec((n_rows, bn), lambda j: (0, j)),
        out_shape=jax.ShapeDtypeStruct((n_rows, n_out), F32),
        name="ada",
    )(c_rows, w_ada, b_ada)


def _ctx_kernel(x_ref, mod_ref, g_ref, wk_ref, wv_ref, wdd_ref, wdec_ref, bdec_ref,
                sf_ref, sb_ref):
    n = x_ref.shape[1]
    nch = n // CHUNK
    hb = _norm_modulate(x_ref[0], g_ref[...], mod_ref[0, 0:1, :], mod_ref[0, 1:2, :])
    la = _log_decay(hb, wdd_ref, wdec_ref, bdec_ref)
    b_f = _chunk_cumsum(la[:, :D_KEY], False)
    b_b = _chunk_cumsum(la[:, D_KEY:], True)
    k = _dot_t(hb, wk_ref[...])
    v = _dot_t(hb, wv_ref[...]).astype(BF16)
    sf_ref[0] = jnp.zeros(sf_ref.shape[1:], F32)
    sb_ref[0] = jnp.zeros(sb_ref.shape[1:], F32)
    mid = CHUNK // 2
    for ci in range(nch):
        rows = slice(ci * CHUNK, (ci + 1) * CHUNK)
        bc = b_f[rows]
        b_mid, b_last = bc[mid - 1:mid], bc[CHUNK - 1:CHUNK]
        kc = (k[rows] * jnp.exp(b_mid - bc)).astype(BF16)
        _state_update_only(kc, v[rows], sf_ref.at[0], jnp.exp(b_last), jnp.exp(b_last - b_mid))
        rows = slice((nch - 1 - ci) * CHUNK, (nch - ci) * CHUNK)
        bc = b_b[rows]
        b_mid, b_last = bc[mid:mid + 1], bc[0:1]
        kc = (k[rows] * jnp.exp(b_mid - bc)).astype(BF16)
        _state_update_only(kc, v[rows], sb_ref.at[0], jnp.exp(b_last), jnp.exp(b_last - b_mid))


def _ctx_call(ctx, mod, g_mix, w_t, w_dec, b_dec, ctx_row):
    bsz, n, _ = ctx.shape
    const = lambda b: (0, 0)
    w_rows = lambda rows, start: pl.BlockSpec((rows, D_MODEL), lambda b: (start // rows, 0),
                                              pipeline_mode=pl.Buffered(1))
    st_spec = pl.BlockSpec((1, HEAD_V, D_KEY), lambda b: (b, 0, 0))
    st_shape = jax.ShapeDtypeStruct((bsz, HEAD_V, D_KEY), F32)
    return pl.pallas_call(
        _ctx_kernel,
        grid=(bsz,),
        in_specs=[
            pl.BlockSpec((1, n, D_MODEL), lambda b: (b, 0, 0)),
            pl.BlockSpec((1, 6, D_MODEL), lambda b: (ctx_row, 0, 0)),
            pl.BlockSpec((1, D_MODEL), const),
            w_rows(D_KEY, ROW_K),
            w_rows(D_VAL, ROW_V),
            w_rows(2 * DECAY_RANK, ROW_DD),
            pl.BlockSpec(w_dec.shape, const),
            pl.BlockSpec(b_dec.shape, const),
        ],
        out_specs=[st_spec, st_spec],
        out_shape=[st_shape, st_shape],
        compiler_params=pltpu.CompilerParams(vmem_limit_bytes=VMEM_LIMIT),
        name="ctx_states",
    )(ctx, mod, g_mix, w_t, w_t, w_t, w_dec, b_dec)


def _proj_bwd_kernel(nt, x0_ref, mod0_ref, xn_ref, modn_ref, g_ref, wt_ref, wdec_ref, bdec_ref,
                     wconv_ref, wco_ref, s0_ref,
                     qf_ref, kf_ref, v_ref, ob_ref, gs_ref, sga_ref, mb_ref, fst_ref,
                     st_ref, p_ref, hb_ref):
    tm = xn_ref.shape[1]
    nch = tm // CHUNK
    mid = CHUNK // 2
    s = pl.program_id(0)
    cur = s % 2

    @pl.when(s == 0)
    def _():
        hb_ref[0] = _norm_modulate(x0_ref[0], g_ref[...], mod0_ref[0, 0:1, :], mod0_ref[0, 1:2, :])

    @pl.when(s % nt == 0)
    def _():
        st_ref[...] = s0_ref[0]

    hb = hb_ref.at[cur]
    width = D_MODEL // PROJ_PARTS
    col_groups = tuple(slice(j * width, (j + 1) * width) for j in range(PROJ_PARTS))
    pos = lax.broadcasted_iota(jnp.int32, (tm, 1), 0) % GRID_W
    tmp = {}

    def proj(row0, cols):
        return _dot_t(hb[...], wt_ref[row0 + cols.start:row0 + cols.stop, :])

    def gate_item(cols):
        def run():
            g = proj(ROW_G, cols)
            gs_ref[0, :, cols] = _silu(g).astype(BF16)
        return run

    def conv_in_item(name, row0, cols):
        def run():
            tmp[name] = proj(row0, cols)
        return run

    def conv_item(cols):
        def run():
            cb = proj(ROW_CB, cols)
            u = tmp.pop("cc") * tmp.pop("ch")
            u_prev = jnp.where(pos == 0, 0.0, pltpu.roll(u, 1, 0))
            u_next = jnp.where(pos == GRID_W - 1, 0.0, pltpu.roll(u, tm - 1, 0))
            uc = (u_prev * wconv_ref[0:1, cols] + u * wconv_ref[1:2, cols]
                  + u_next * wconv_ref[2:3, cols])
            p_ref[:, cols] = (cb * uc).astype(BF16)
        return run

    def conv_out_item(cols):
        def run():
            tmp["yc"] = _dot(p_ref[...], wco_ref[:, cols])
        return run

    def merge_gate_item(cols):
        def run():
            gb = proj(ROW_GB, cols)
            mb_ref[0, :, cols] = (_sigmoid(gb) * tmp.pop("yc")).astype(BF16)
        return run

    def gla_gate_item(cols):
        def run():
            ga = proj(ROW_GA, cols)
            sga_ref[0, :, cols] = _sigmoid(ga).astype(BF16)
        return run

    def next_input_item():
        hb_ref[1 - cur] = _norm_modulate(xn_ref[0], g_ref[...], modn_ref[0, 0:1, :],
                                         modn_ref[0, 1:2, :])

    items = [gate_item(c) for c in col_groups]
    for c in col_groups:
        items += [conv_in_item("cc", ROW_CC, c), conv_in_item("ch", ROW_CH, c), conv_item(c)]
    items += [next_input_item]
    for c in col_groups:
        items += [conv_out_item(c), merge_gate_item(c)]
    items += [gla_gate_item(c) for c in col_groups]
    fill, drain = _make_fill(items, 1)

    la = _log_decay(hb[...], wt_ref.at[ROW_DD:ROW_CB, :], wdec_ref, bdec_ref)
    fill()
    b_f = _chunk_cumsum(la[:, :D_KEY], False)
    b_b = _chunk_cumsum(la[:, D_KEY:], True)
    fill()

    q = proj(ROW_Q, slice(0, D_KEY)) * (HEAD_K ** -0.5)
    k = proj(ROW_K, slice(0, D_KEY))
    vb = proj(ROW_V, slice(0, D_VAL)).astype(BF16)
    v_ref[0] = vb

    for ci in reversed(range(nch)):
        rows = slice(ci * CHUNK, (ci + 1) * CHUNK)
        bc = b_f[rows]
        b_mid, b_last = bc[mid - 1:mid], bc[CHUNK - 1:CHUNK]
        d = bc - b_mid
        qf_ref[0, rows, :] = (q[rows] * jnp.exp(d)).astype(BF16)
        kf_ref[0, rows, :] = (k[rows] * jnp.exp(-d)).astype(BF16)
        fst_ref[0, 0, ci:ci + 1, :] = jnp.concatenate([b_mid, b_last], axis=-1)
        bc = b_b[rows]
        b_mid, b_last = bc[mid:mid + 1], bc[0:1]
        d = bc - b_mid
        qc = (q[rows] * jnp.exp(d)).astype(BF16)
        kc = (k[rows] * jnp.exp(-d)).astype(BF16)
        ob_ref[0, rows, :] = _gla_chunk(qc, kc, vb[rows], st_ref, jnp.exp(b_mid),
                                        jnp.exp(b_last), jnp.exp(b_last - b_mid), True, fill)
    drain()


def _proj_bwd_call(x, mod, g_mix, w_t, w_dec, b_dec, w_conv, w_co, s0):
    bsz, t, _ = x.shape
    tm = TM_REV
    nt = t // tm
    nch = tm // CHUNK
    steps = bsz * nt
    cur = lambda s: (s // nt, nt - 1 - s % nt, 0)
    nxt = lambda s: cur(jnp.minimum(s + 1, steps - 1))
    const = lambda s: (0, 0)
    tok = lambda width: pl.BlockSpec((1, tm, width), cur)
    tok_shape = lambda width, dt: jax.ShapeDtypeStruct((bsz, t, width), dt)
    return pl.pallas_call(
        functools.partial(_proj_bwd_kernel, nt),
        grid=(steps,),
        in_specs=[
            pl.BlockSpec((1, tm, D_MODEL), lambda s: (0, nt - 1, 0)),
            pl.BlockSpec((1, 6, D_MODEL), lambda s: (0, 0, 0)),
            pl.BlockSpec((1, tm, D_MODEL), nxt),
            pl.BlockSpec((1, 6, D_MODEL), lambda s: (jnp.minimum(s + 1, steps - 1) // nt, 0, 0)),
            pl.BlockSpec((1, D_MODEL), const),
            pl.BlockSpec(w_t.shape, const),
            pl.BlockSpec(w_dec.shape, const),
            pl.BlockSpec(b_dec.shape, const),
            pl.BlockSpec(w_conv.shape, const),
            pl.BlockSpec(w_co.shape, const),
            pl.BlockSpec((1, HEAD_V, D_KEY), lambda s: (s // nt, 0, 0)),
        ],
        out_specs=[
            tok(D_KEY), tok(D_KEY), tok(D_VAL), tok(D_VAL), tok(D_VAL), tok(D_MODEL), tok(D_MODEL),
            pl.BlockSpec((1, 1, nch, 2 * D_KEY), lambda s: (s // nt, nt - 1 - s % nt, 0, 0)),
        ],
        out_shape=[
            tok_shape(D_KEY, BF16), tok_shape(D_KEY, BF16), tok_shape(D_VAL, BF16),
            tok_shape(D_VAL, F32), tok_shape(D_VAL, BF16), tok_shape(D_MODEL, BF16),
            tok_shape(D_MODEL, BF16),
            jax.ShapeDtypeStruct((bsz, nt, nch, 2 * D_KEY), F32),
        ],
        scratch_shapes=[pltpu.VMEM((HEAD_V, D_KEY), F32), pltpu.VMEM((tm, D_MODEL), BF16),
                        pltpu.VMEM((2, tm, D_MODEL), BF16)],
        compiler_params=pltpu.CompilerParams(
            dimension_semantics=("arbitrary",), vmem_limit_bytes=VMEM_LIMIT),
        name="proj_bwd",
    )(x, mod, x, mod, g_mix, w_t, w_dec, b_dec, w_conv, w_co, s0)


def _fwd_ffn_kernel(nt, steps, qf_ref, kf_ref, v_ref, ob_ref, gs_ref, fst_ref, s0_ref, gh_ref,
                    x_ref, sga_ref, mb_ref, mod_ref, wgo_ref, wmo_ref, gffn_ref, win_ref, wout_ref,
                    gfin_ref, o_ref, st_ref, a_ref):
    tm = x_ref.shape[1]
    nch = tm // CHUNK
    s = pl.program_id(0)
    t = jnp.minimum(s, steps - 1)

    @pl.when(s == 0)
    def _():
        a_ref[...] = jnp.zeros(a_ref.shape, a_ref.dtype)

    @pl.when(t % nt == 0)
    def _():
        st_ref[...] = s0_ref[0]

    tmp = {}

    def mix_item():
        merged = sga_ref[0].astype(F32) * tmp.pop("y_gla") + mb_ref[0].astype(F32)
        mix = _dot(merged.astype(BF16), wmo_ref[...])
        x1 = x_ref[0] + mod_ref[0, 2:3, :] * mix
        tmp["x1"] = x1
        tmp["hb"] = _norm_modulate(x1, gffn_ref[...], mod_ref[0, 3:4, :], mod_ref[0, 4:5, :])

    def ffn_in_item(name, cols):
        def run():
            tmp[name] = _dot(tmp["hb"], win_ref[:, cols])
        return run

    def ffn_out_item(cols):
        def run():
            hidden = (_silu(tmp.pop("a")) * tmp.pop("b")).astype(BF16)
            part = _dot(hidden, wout_ref[cols, :])
            tmp["y"] = part if "y" not in tmp else tmp["y"] + part
        return run

    items = [mix_item]
    start = 0
    for width in FF_GROUPS:
        cols = slice(start, start + width)
        gate_cols = slice(D_FF + start, D_FF + start + width)
        items += [ffn_in_item("a", cols), ffn_in_item("b", gate_cols), ffn_out_item(cols)]
        start += width
    fill, drain = _make_fill(items, -(-len(items) // (3 * nch)))

    tmp["y_gla"] = _dot(a_ref[...], wgo_ref[...])

    for ci in range(nch):
        rows = slice(ci * CHUNK, (ci + 1) * CHUNK)
        b_mid = fst_ref[0, 0, ci:ci + 1, 0:D_KEY]
        b_last = fst_ref[0, 0, ci:ci + 1, D_KEY:]
        o = _gla_chunk(qf_ref[0, rows, :], kf_ref[0, rows, :], v_ref[0, rows, :], st_ref,
                       jnp.exp(b_mid), jnp.exp(b_last), jnp.exp(b_last - b_mid), False, fill)
        o = o + ob_ref[0, rows, :]
        for h in range(N_HEADS):
            vs = slice(h * HEAD_V, (h + 1) * HEAD_V)
            a_ref[rows, vs] = (_rms(o[:, vs]) * gh_ref[...]
                               * gs_ref[0, rows, vs].astype(F32)).astype(BF16)
    drain()

    x2 = tmp["x1"] + mod_ref[0, 5:6, :] * tmp["y"]
    o_ref[0] = _rms(x2) * gfin_ref[...]


def _fwd_ffn_call(x, mod, qf, kf, v, ob, gs, sga, mb, fst, g_head, w_go, w_mo, s0, g_ffn, w_in,
                  w_out, g_final):
    bsz, t, _ = x.shape
    tm = TM_FWD
    nt = t // tm
    nch = tm // CHUNK
    steps = bsz * nt
    fst = fst.reshape(bsz, nt, nch, 2 * D_KEY)
    scan = lambda s: (jnp.minimum(s, steps - 1) // nt, jnp.minimum(s, steps - 1) % nt, 0)
    rest = lambda s: (jnp.maximum(s - 1, 0) // nt, jnp.maximum(s - 1, 0) % nt, 0)
    const = lambda s: (0, 0)
    scan_tok = lambda width: pl.BlockSpec((1, tm, width), scan)
    rest_tok = lambda width: pl.BlockSpec((1, tm, width), rest)
    return pl.pallas_call(
        functools.partial(_fwd_ffn_kernel, nt, steps),
        grid=(steps + 1,),
        in_specs=[
            scan_tok(D_KEY), scan_tok(D_KEY), scan_tok(D_VAL), scan_tok(D_VAL), scan_tok(D_VAL),
            pl.BlockSpec((1, 1, nch, 2 * D_KEY), lambda s: scan(s) + (0,)),
            pl.BlockSpec((1, HEAD_V, D_KEY), lambda s: (jnp.minimum(s, steps - 1) // nt, 0, 0)),
            pl.BlockSpec((1, HEAD_V), const),
            rest_tok(D_MODEL), rest_tok(D_MODEL), rest_tok(D_MODEL),
            pl.BlockSpec((1, 6, D_MODEL), lambda s: (jnp.maximum(s - 1, 0) // nt, 0, 0)),
            pl.BlockSpec(w_go.shape, const),
            pl.BlockSpec(w_mo.shape, const),
            pl.BlockSpec((1, D_MODEL), const),
            pl.BlockSpec(w_in.shape, const),
            pl.BlockSpec(w_out.shape, const),
            pl.BlockSpec((1, D_MODEL), const),
        ],
        out_specs=rest_tok(D_MODEL),
        out_shape=jax.ShapeDtypeStruct((bsz, t, D_MODEL), F32),
        scratch_shapes=[pltpu.VMEM((HEAD_V, D_KEY), F32), pltpu.VMEM((tm, D_VAL), BF16)],
        compiler_params=pltpu.CompilerParams(
            dimension_semantics=("arbitrary",), vmem_limit_bytes=VMEM_LIMIT),
        name="fwd_ffn",
    )(qf, kf, v, ob, gs, fst, s0, g_head, x, sga, mb, mod, w_go, w_mo, g_ffn, w_in, w_out, g_final)


def kernel(x, c, ctx, c_ctx, w_ada, b_ada, g_mix, w_in, w_dec_up, b_dec, g_head, w_gla_out,
           w_conv, w_conv_out, w_mix_out, g_ffn, w_ffn_in, w_ffn_out, g_final):
    assert w_ada.shape[0] == 1, "single-layer block"
    bsz = x.shape[0]
    assert x.shape[1] % TM_REV == 0 and x.shape[1] % TM_FWD == 0
    assert ctx.shape[1] % CHUNK == 0 and sum(FF_GROUPS) == D_FF

    assert w_in.shape[2] == N_IN
    w_t = jnp.transpose(w_in[0]).astype(BF16)
    zeros = jnp.zeros((DECAY_RANK, D_KEY), F32)
    w_dec = jnp.concatenate([jnp.concatenate([w_dec_up[0, 0], zeros], axis=1),
                             jnp.concatenate([zeros, w_dec_up[0, 1]], axis=1)], axis=0).astype(BF16)
    b_dec2 = b_dec[0].reshape(1, 2 * D_KEY)

    n_rows = -(-(bsz + 1) // 8) * 8
    c_rows = jnp.zeros((n_rows, D_MODEL), F32).at[:bsz].set(c).at[bsz].set(c_ctx)
    mod = _ada_call(c_rows, w_ada[0], b_ada).reshape(n_rows, 6, D_MODEL)

    g_mix2 = g_mix.reshape(1, D_MODEL)
    s_cf, s_cb = _ctx_call(ctx, mod, g_mix2, w_t, w_dec, b_dec2, bsz)
    qf, kf, v, ob, gs, sga, mb, fst = _proj_bwd_call(
        x, mod, g_mix2, w_t, w_dec, b_dec2, w_conv[0], w_conv_out[0].astype(BF16), s_cb)
    return _fwd_ffn_call(x, mod, qf, kf, v, ob, gs, sga, mb, fst, g_head.reshape(1, HEAD_V),
                         w_gla_out[0].astype(BF16), w_mix_out[0].astype(BF16), s_cf,
                         g_ffn.reshape(1, D_MODEL), w_ffn_in[0].astype(BF16),
                         w_ffn_out[0].astype(BF16), g_final.reshape(1, D_MODEL))
```

```python
import functools

import jax
import jax.numpy as jnp
from jax import lax
from jax.experimental import pallas as pl
from jax.experimental.pallas import tpu as pltpu

D_MODEL = 1024
N_HEADS = 4
D_KEY = 512
D_VAL = 1024
HEAD_K = D_KEY // N_HEADS
HEAD_V = D_VAL // N_HEADS
DECAY_RANK = 16
DECAY_TAU = 16.0
CHUNK = 64
GRID_W = 64
D_FF = 2816
EPS = 1e-6

LANES = 128
MXU_TILE = 256
TM_REV = 512
TM_FWD = 512
PROJ_PARTS = 4
COL_Q, COL_K, COL_V, COL_G = 0, D_KEY, 2 * D_KEY, 2 * D_KEY + D_VAL
COL_DD = COL_G + D_VAL
COL_CB = COL_DD + D_MODEL
COL_CC, COL_CH, COL_GA, COL_GB = (COL_CB + D_MODEL, COL_CB + 2 * D_MODEL, COL_CB + 3 * D_MODEL,
                                  COL_CB + 4 * D_MODEL)
N_PACK = COL_GB + D_MODEL
N_IN = N_PACK - (COL_CB - COL_DD) + 2 * DECAY_RANK
PACK_COLS = 512
FF_GROUPS = (512, 512, 512, 512, 512, 256)
VMEM_LIMIT = 60 * 1024 * 1024

BF16 = jnp.bfloat16
F32 = jnp.float32

NT_DIMS = (((1,), (1,)), ((), ()))
TN_DIMS = (((0,), (0,)), ((), ()))


def _dot(a, b):
    return jnp.dot(a, b, preferred_element_type=F32)


def _sigmoid(x):
    return 1.0 / (1.0 + jnp.exp(-x))


def _silu(x):
    return x * _sigmoid(x)


def _log_sigmoid(x):
    return jnp.minimum(x, 0.0) - jnp.log(1.0 + jnp.exp(-jnp.abs(x)))


def _rms(x):
    return x * lax.rsqrt(jnp.mean(x * x, axis=-1, keepdims=True) + EPS)


def _norm_modulate(x, g, shift, scale):
    return (_rms(x) * g * (1.0 + scale) + shift).astype(BF16)


def _cumsum_matrix(n, upper):
    r = lax.broadcasted_iota(jnp.int32, (n, n), 0)
    c = lax.broadcasted_iota(jnp.int32, (n, n), 1)
    same = (r // CHUNK) == (c // CHUNK)
    tri = (c >= r) if upper else (c <= r)
    return jnp.where(same & tri, 1.0, 0.0).astype(BF16)


def _chunk_cumsum(la, upper):
    rows = min(la.shape[0], MXU_TILE)
    mat = _cumsum_matrix(rows, upper)
    hi = la.astype(BF16)
    lo = (la - hi.astype(F32)).astype(BF16)
    parts = [_dot(mat, hi[r:r + rows]) + _dot(mat, lo[r:r + rows])
             for r in range(0, la.shape[0], rows)]
    return parts[0] if len(parts) == 1 else jnp.concatenate(parts, axis=0)


def _log_decay(hb, wdd_ref, wdec_ref, bdec_ref):
    dd = _dot(hb, wdd_ref[...]).astype(BF16)
    zd = _dot(dd, wdec_ref[...]) + bdec_ref[...]
    return _log_sigmoid(zd) * (1.0 / DECAY_TAU)


def _kv_all_heads(kc, vc):
    v_stack = jnp.concatenate([vc[:, h * HEAD_V:(h + 1) * HEAD_V] for h in range(N_HEADS)], axis=0)
    r = lax.broadcasted_iota(jnp.int32, (N_HEADS * CHUNK, D_KEY), 0) // CHUNK
    c = lax.broadcasted_iota(jnp.int32, (N_HEADS * CHUNK, D_KEY), 1) // HEAD_K
    k_blk = jnp.where(r == c, jnp.concatenate([kc] * N_HEADS, axis=0), jnp.zeros((), kc.dtype))
    return lax.dot_general(v_stack, k_blk, TN_DIMS, preferred_element_type=F32)


def _make_fill(items, per_gap):
    queue = list(reversed(items))

    def fill(n=per_gap):
        for _ in range(n):
            if queue:
                queue.pop()()

    def drain():
        while queue:
            queue.pop()()

    return fill, drain


def _no_fill():
    pass


def _gla_chunk(qc, kc, vc, st_ref, e_mid, e_last, e_gap, upper, fill=_no_fill):
    r = lax.broadcasted_iota(jnp.int32, (CHUNK, CHUNK), 0)
    c = lax.broadcasted_iota(jnp.int32, (CHUNK, CHUNK), 1)
    keep = (c >= r) if upper else (c <= r)
    heads = [(slice(h * HEAD_K, (h + 1) * HEAD_K), slice(h * HEAD_V, (h + 1) * HEAD_V))
             for h in range(N_HEADS)]
    scores = [lax.dot_general(qc[:, ks], kc[:, ks], NT_DIMS, preferred_element_type=F32)
              for ks, _ in heads]
    fill()
    st = st_ref[...]
    st_in = (st * e_mid).astype(BF16)
    outs = []
    for (ks, vs), s in zip(heads, scores):
        a = jnp.where(keep, s, 0.0).astype(BF16)
        outs.append(_dot(a, vc[:, vs])
                    + lax.dot_general(qc[:, ks], st_in[:, ks], NT_DIMS, preferred_element_type=F32))
    fill()
    st_ref[...] = st * e_last + _kv_all_heads(kc, vc) * e_gap
    fill()
    return jnp.concatenate(outs, axis=-1)


def _state_update_only(kc, vc, st_ref, e_last, e_gap):
    st_ref[...] = st_ref[...] * e_last + _kv_all_heads(kc, vc) * e_gap


def _pack_kernel(a_ref, b_ref, o_ref):
    j = pl.program_id(0)
    dd_block = COL_DD // PACK_COLS
    first_shifted = COL_CB // PACK_COLS
    dd_rows = 2 * DECAY_RANK

    @pl.when(j < dd_block)
    def _():
        o_ref[...] = a_ref[...].T.astype(BF16)

    @pl.when(j == dd_block)
    def _():
        row = lax.broadcasted_iota(jnp.int32, a_ref.shape, 0)
        o_ref[...] = jnp.where(row < dd_rows, a_ref[...], 0.0).T.astype(BF16)

    @pl.when((j > dd_block) & (j < first_shifted))
    def _():
        o_ref[...] = jnp.zeros(o_ref.shape, BF16)

    @pl.when(j >= first_shifted)
    def _():
        src = jnp.concatenate([a_ref[dd_rows:, :], b_ref[...]], axis=0)
        o_ref[...] = src.T.astype(BF16)


def _pack_call(w_t):
    dd_block = COL_DD // PACK_COLS
    first_shifted = COL_CB // PACK_COLS
    dd_rows = 2 * DECAY_RANK
    a_idx = lambda j: jnp.where(j <= dd_block, j, jnp.maximum(j - (first_shifted - dd_block), dd_block))
    return pl.pallas_call(
        _pack_kernel,
        grid=(N_PACK // PACK_COLS,),
        in_specs=[
            pl.BlockSpec((PACK_COLS, D_MODEL), lambda j: (a_idx(j), 0)),
            pl.BlockSpec((dd_rows, D_MODEL), lambda j: ((a_idx(j) + 1) * (PACK_COLS // dd_rows), 0)),
        ],
        out_specs=pl.BlockSpec((D_MODEL, PACK_COLS), lambda j: (0, j)),
        out_shape=jax.ShapeDtypeStruct((D_MODEL, N_PACK), BF16),
        name="pack_w_in",
    )(w_t, w_t)


def _ada_kernel(c_ref, w_ref, b_ref, o_ref):
    s = _silu(c_ref[...]).astype(BF16)
    o_ref[...] = _dot(s, w_ref[...].astype(BF16)) + b_ref[...]


def _ada_call(c_rows, w_ada, b_ada):
    n_rows = c_rows.shape[0]
    n_out = w_ada.shape[1]
    bn = 512
    return pl.pallas_call(
        _ada_kernel,
        grid=(n_out // bn,),
        in_specs=[
            pl.BlockSpec((n_rows, D_MODEL), lambda j: (0, 0)),
            pl.BlockSpec((D_MODEL, bn), lambda j: (0, j)),
            pl.BlockSpec((1, bn), lambda j: (0, j)),
        ],
        out_specs=pl.BlockSpec((n_rows, bn), lambda j: (0, j)),
        out_shape=jax.ShapeDtypeStruct((n_rows, n_out), F32),
        name="ada",
    )(c_rows, w_ada, b_ada)


def _ctx_kernel(x_ref, mod_ref, g_ref, wk_ref, wv_ref, wdd_ref, wdec_ref, bdec_ref,
                sf_ref, sb_ref):
    n = x_ref.shape[1]
    nch = n // CHUNK
    hb = _norm_modulate(x_ref[0], g_ref[...], mod_ref[0, 0:1, :], mod_ref[0, 1:2, :])
    la = _log_decay(hb, wdd_ref, wdec_ref, bdec_ref)
    b_f = _chunk_cumsum(la[:, :D_KEY], False)
    b_b = _chunk_cumsum(la[:, D_KEY:], True)
    k = _dot(hb, wk_ref[...])
    v = _dot(hb, wv_ref[...]).astype(BF16)
    sf_ref[0] = jnp.zeros(sf_ref.shape[1:], F32)
    sb_ref[0] = jnp.zeros(sb_ref.shape[1:], F32)
    mid = CHUNK // 2
    for ci in range(nch):
        rows = slice(ci * CHUNK, (ci + 1) * CHUNK)
        bc = b_f[rows]
        b_mid, b_last = bc[mid - 1:mid], bc[CHUNK - 1:CHUNK]
        kc = (k[rows] * jnp.exp(b_mid - bc)).astype(BF16)
        _state_update_only(kc, v[rows], sf_ref.at[0], jnp.exp(b_last), jnp.exp(b_last - b_mid))
        rows = slice((nch - 1 - ci) * CHUNK, (nch - ci) * CHUNK)
        bc = b_b[rows]
        b_mid, b_last = bc[mid:mid + 1], bc[0:1]
        kc = (k[rows] * jnp.exp(b_mid - bc)).astype(BF16)
        _state_update_only(kc, v[rows], sb_ref.at[0], jnp.exp(b_last), jnp.exp(b_last - b_mid))


def _ctx_call(ctx, mod, g_mix, w_pack, w_dec, b_dec, ctx_row):
    bsz, n, _ = ctx.shape
    const = lambda b: (0, 0)
    w_cols = lambda width, start: pl.BlockSpec((D_MODEL, width), lambda b: (0, start // width),
                                               pipeline_mode=pl.Buffered(1))
    st_spec = pl.BlockSpec((1, HEAD_V, D_KEY), lambda b: (b, 0, 0))
    st_shape = jax.ShapeDtypeStruct((bsz, HEAD_V, D_KEY), F32)
    return pl.pallas_call(
        _ctx_kernel,
        grid=(bsz,),
        in_specs=[
            pl.BlockSpec((1, n, D_MODEL), lambda b: (b, 0, 0)),
            pl.BlockSpec((1, 6, D_MODEL), lambda b: (ctx_row, 0, 0)),
            pl.BlockSpec((1, D_MODEL), const),
            w_cols(D_KEY, COL_K),
            w_cols(D_VAL, COL_V),
            w_cols(LANES, COL_DD),
            pl.BlockSpec(w_dec.shape, const),
            pl.BlockSpec(b_dec.shape, const),
        ],
        out_specs=[st_spec, st_spec],
        out_shape=[st_shape, st_shape],
        compiler_params=pltpu.CompilerParams(vmem_limit_bytes=VMEM_LIMIT),
        name="ctx_states",
    )(ctx, mod, g_mix, w_pack, w_pack, w_pack, w_dec, b_dec)


def _proj_bwd_kernel(nt, x0_ref, mod0_ref, xn_ref, modn_ref, g_ref, wqk_ref, wv_ref, wg_ref, wdd_ref,
                     wdec_ref, bdec_ref, wcb_ref, wcc_ref, wch_ref, wga_ref, wgb_ref, wconv_ref,
                     wco_ref, s0_ref,
                     qf_ref, kf_ref, v_ref, ob_ref, gs_ref, sga_ref, mb_ref, fst_ref,
                     st_ref, p_ref, hb_ref):
    tm = xn_ref.shape[1]
    nch = tm // CHUNK
    mid = CHUNK // 2
    s = pl.program_id(0)
    cur = s % 2

    @pl.when(s == 0)
    def _():
        hb_ref[0] = _norm_modulate(x0_ref[0], g_ref[...], mod0_ref[0, 0:1, :], mod0_ref[0, 1:2, :])

    @pl.when(s % nt == 0)
    def _():
        st_ref[...] = s0_ref[0]

    hb = hb_ref.at[cur]
    width = D_MODEL // PROJ_PARTS
    col_groups = tuple(slice(j * width, (j + 1) * width) for j in range(PROJ_PARTS))
    pos = lax.broadcasted_iota(jnp.int32, (tm, 1), 0) % GRID_W
    tmp = {}

    def proj(w_ref, cols):
        return _dot(hb[...], w_ref[:, cols])

    def gate_item(cols):
        def run():
            g = proj(wg_ref, cols)
            gs_ref[0, :, cols] = _silu(g).astype(BF16)
        return run

    def conv_in_item(name, w_ref, cols):
        def run():
            tmp[name] = proj(w_ref, cols)
        return run

    def conv_item(cols):
        def run():
            cb = proj(wcb_ref, cols)
            u = tmp.pop("cc") * tmp.pop("ch")
            u_prev = jnp.where(pos == 0, 0.0, pltpu.roll(u, 1, 0))
            u_next = jnp.where(pos == GRID_W - 1, 0.0, pltpu.roll(u, tm - 1, 0))
            uc = (u_prev * wconv_ref[0:1, cols] + u * wconv_ref[1:2, cols]
                  + u_next * wconv_ref[2:3, cols])
            p_ref[:, cols] = (cb * uc).astype(BF16)
        return run

    def conv_out_item(cols):
        def run():
            tmp["yc"] = _dot(p_ref[...], wco_ref[:, cols])
        return run

    def merge_gate_item(cols):
        def run():
            gb = proj(wgb_ref, cols)
            mb_ref[0, :, cols] = (_sigmoid(gb) * tmp.pop("yc")).astype(BF16)
        return run

    def gla_gate_item(cols):
        def run():
            ga = proj(wga_ref, cols)
            sga_ref[0, :, cols] = _sigmoid(ga).astype(BF16)
        return run

    def next_input_item():
        hb_ref[1 - cur] = _norm_modulate(xn_ref[0], g_ref[...], modn_ref[0, 0:1, :],
                                         modn_ref[0, 1:2, :])

    items = [gate_item(c) for c in col_groups]
    for c in col_groups:
        items += [conv_in_item("cc", wcc_ref, c), conv_in_item("ch", wch_ref, c), conv_item(c)]
    items += [next_input_item]
    for c in col_groups:
        items += [conv_out_item(c), merge_gate_item(c)]
    items += [gla_gate_item(c) for c in col_groups]
    fill, drain = _make_fill(items, 1)

    la = _log_decay(hb[...], wdd_ref, wdec_ref, bdec_ref)
    fill()
    b_f = _chunk_cumsum(la[:, :D_KEY], False)
    b_b = _chunk_cumsum(la[:, D_KEY:], True)
    fill()

    q = proj(wqk_ref, slice(0, D_KEY)) * (HEAD_K ** -0.5)
    k = proj(wqk_ref, slice(D_KEY, 2 * D_KEY))
    vb = proj(wv_ref, slice(0, D_VAL)).astype(BF16)
    v_ref[0] = vb

    for ci in reversed(range(nch)):
        rows = slice(ci * CHUNK, (ci + 1) * CHUNK)
        bc = b_f[rows]
        b_mid, b_last = bc[mid - 1:mid], bc[CHUNK - 1:CHUNK]
        d = bc - b_mid
        qf_ref[0, rows, :] = (q[rows] * jnp.exp(d)).astype(BF16)
        kf_ref[0, rows, :] = (k[rows] * jnp.exp(-d)).astype(BF16)
        fst_ref[0, 0, ci:ci + 1, :] = jnp.concatenate([b_mid, b_last], axis=-1)
        bc = b_b[rows]
        b_mid, b_last = bc[mid:mid + 1], bc[0:1]
        d = bc - b_mid
        qc = (q[rows] * jnp.exp(d)).astype(BF16)
        kc = (k[rows] * jnp.exp(-d)).astype(BF16)
        ob_ref[0, rows, :] = _gla_chunk(qc, kc, vb[rows], st_ref, jnp.exp(b_mid),
                                        jnp.exp(b_last), jnp.exp(b_last - b_mid), True, fill)
    drain()


def _proj_bwd_call(x, mod, g_mix, w_pack, w_dec, b_dec, w_conv, w_co, s0):
    bsz, t, _ = x.shape
    tm = TM_REV
    nt = t // tm
    nch = tm // CHUNK
    steps = bsz * nt
    cur = lambda s: (s // nt, nt - 1 - s % nt, 0)
    nxt = lambda s: cur(jnp.minimum(s + 1, steps - 1))
    const = lambda s: (0, 0)
    w_cols = lambda width, start: pl.BlockSpec((D_MODEL, width), lambda s: (0, start // width),
                                               pipeline_mode=pl.Buffered(1))
    tok = lambda width: pl.BlockSpec((1, tm, width), cur)
    tok_shape = lambda width, dt: jax.ShapeDtypeStruct((bsz, t, width), dt)
    return pl.pallas_call(
        functools.partial(_proj_bwd_kernel, nt),
        grid=(steps,),
        in_specs=[
            pl.BlockSpec((1, tm, D_MODEL), lambda s: (0, nt - 1, 0)),
            pl.BlockSpec((1, 6, D_MODEL), lambda s: (0, 0, 0)),
            pl.BlockSpec((1, tm, D_MODEL), nxt),
            pl.BlockSpec((1, 6, D_MODEL), lambda s: (jnp.minimum(s + 1, steps - 1) // nt, 0, 0)),
            pl.BlockSpec((1, D_MODEL), const),
            w_cols(2 * D_KEY, COL_Q), w_cols(D_VAL, COL_V), w_cols(D_VAL, COL_G), w_cols(LANES, COL_DD),
            pl.BlockSpec(w_dec.shape, const),
            pl.BlockSpec(b_dec.shape, const),
            w_cols(D_MODEL, COL_CB), w_cols(D_MODEL, COL_CC), w_cols(D_MODEL, COL_CH),
            w_cols(D_MODEL, COL_GA), w_cols(D_MODEL, COL_GB),
            pl.BlockSpec(w_conv.shape, const),
            pl.BlockSpec(w_co.shape, const),
            pl.BlockSpec((1, HEAD_V, D_KEY), lambda s: (s // nt, 0, 0)),
        ],
        out_specs=[
            tok(D_KEY), tok(D_KEY), tok(D_VAL), tok(D_VAL), tok(D_VAL), tok(D_MODEL), tok(D_MODEL),
            pl.BlockSpec((1, 1, nch, 2 * D_KEY), lambda s: (s // nt, nt - 1 - s % nt, 0, 0)),
        ],
        out_shape=[
            tok_shape(D_KEY, BF16), tok_shape(D_KEY, BF16), tok_shape(D_VAL, BF16),
            tok_shape(D_VAL, F32), tok_shape(D_VAL, BF16), tok_shape(D_MODEL, BF16),
            tok_shape(D_MODEL, BF16),
            jax.ShapeDtypeStruct((bsz, nt, nch, 2 * D_KEY), F32),
        ],
        scratch_shapes=[pltpu.VMEM((HEAD_V, D_KEY), F32), pltpu.VMEM((tm, D_MODEL), BF16),
                        pltpu.VMEM((2, tm, D_MODEL), BF16)],
        compiler_params=pltpu.CompilerParams(
            dimension_semantics=("arbitrary",), vmem_limit_bytes=VMEM_LIMIT),
        name="proj_bwd",
    )(x, mod, x, mod, g_mix, w_pack, w_pack, w_pack, w_pack, w_dec, b_dec,
      w_pack, w_pack, w_pack, w_pack, w_pack, w_conv, w_co, s0)


def _fwd_ffn_kernel(nt, steps, qf_ref, kf_ref, v_ref, ob_ref, gs_ref, fst_ref, s0_ref, gh_ref,
                    x_ref, sga_ref, mb_ref, mod_ref, wgo_ref, wmo_ref, gffn_ref, win_ref, wout_ref,
                    gfin_ref, o_ref, st_ref, a_ref):
    tm = x_ref.shape[1]
    nch = tm // CHUNK
    s = pl.program_id(0)
    t = jnp.minimum(s, steps - 1)

    @pl.when(s == 0)
    def _():
        a_ref[...] = jnp.zeros(a_ref.shape, a_ref.dtype)

    @pl.when(t % nt == 0)
    def _():
        st_ref[...] = s0_ref[0]

    tmp = {}

    def mix_item():
        merged = sga_ref[0].astype(F32) * tmp.pop("y_gla") + mb_ref[0].astype(F32)
        mix = _dot(merged.astype(BF16), wmo_ref[...])
        x1 = x_ref[0] + mod_ref[0, 2:3, :] * mix
        tmp["x1"] = x1
        tmp["hb"] = _norm_modulate(x1, gffn_ref[...], mod_ref[0, 3:4, :], mod_ref[0, 4:5, :])

    def ffn_in_item(name, cols):
        def run():
            tmp[name] = _dot(tmp["hb"], win_ref[:, cols])
        return run

    def ffn_out_item(cols):
        def run():
            hidden = (_silu(tmp.pop("a")) * tmp.pop("b")).astype(BF16)
            part = _dot(hidden, wout_ref[cols, :])
            tmp["y"] = part if "y" not in tmp else tmp["y"] + part
        return run

    items = [mix_item]
    start = 0
    for width in FF_GROUPS:
        cols = slice(start, start + width)
        gate_cols = slice(D_FF + start, D_FF + start + width)
        items += [ffn_in_item("a", cols), ffn_in_item("b", gate_cols), ffn_out_item(cols)]
        start += width
    fill, drain = _make_fill(items, -(-len(items) // (3 * nch)))

    tmp["y_gla"] = _dot(a_ref[...], wgo_ref[...])

    for ci in range(nch):
        rows = slice(ci * CHUNK, (ci + 1) * CHUNK)
        b_mid = fst_ref[0, 0, ci:ci + 1, 0:D_KEY]
        b_last = fst_ref[0, 0, ci:ci + 1, D_KEY:]
        o = _gla_chunk(qf_ref[0, rows, :], kf_ref[0, rows, :], v_ref[0, rows, :], st_ref,
                       jnp.exp(b_mid), jnp.exp(b_last), jnp.exp(b_last - b_mid), False, fill)
        o = o + ob_ref[0, rows, :]
        for h in range(N_HEADS):
            vs = slice(h * HEAD_V, (h + 1) * HEAD_V)
            a_ref[rows, vs] = (_rms(o[:, vs]) * gh_ref[...]
                               * gs_ref[0, rows, vs].astype(F32)).astype(BF16)
    drain()

    x2 = tmp["x1"] + mod_ref[0, 5:6, :] * tmp["y"]
    o_ref[0] = _rms(x2) * gfin_ref[...]


def _fwd_ffn_call(x, mod, qf, kf, v, ob, gs, sga, mb, fst, g_head, w_go, w_mo, s0, g_ffn, w_in,
                  w_out, g_final):
    bsz, t, _ = x.shape
    tm = TM_FWD
    nt = t // tm
    nch = tm // CHUNK
    steps = bsz * nt
    fst = fst.reshape(bsz, nt, nch, 2 * D_KEY)
    scan = lambda s: (jnp.minimum(s, steps - 1) // nt, jnp.minimum(s, steps - 1) % nt, 0)
    rest = lambda s: (jnp.maximum(s - 1, 0) // nt, jnp.maximum(s - 1, 0) % nt, 0)
    const = lambda s: (0, 0)
    scan_tok = lambda width: pl.BlockSpec((1, tm, width), scan)
    rest_tok = lambda width: pl.BlockSpec((1, tm, width), rest)
    return pl.pallas_call(
        functools.partial(_fwd_ffn_kernel, nt, steps),
        grid=(steps + 1,),
        in_specs=[
            scan_tok(D_KEY), scan_tok(D_KEY), scan_tok(D_VAL), scan_tok(D_VAL), scan_tok(D_VAL),
            pl.BlockSpec((1, 1, nch, 2 * D_KEY), lambda s: scan(s) + (0,)),
            pl.BlockSpec((1, HEAD_V, D_KEY), lambda s: (jnp.minimum(s, steps - 1) // nt, 0, 0)),
            pl.BlockSpec((1, HEAD_V), const),
            rest_tok(D_MODEL), rest_tok(D_MODEL), rest_tok(D_MODEL),
            pl.BlockSpec((1, 6, D_MODEL), lambda s: (jnp.maximum(s - 1, 0) // nt, 0, 0)),
            pl.BlockSpec(w_go.shape, const),
            pl.BlockSpec(w_mo.shape, const),
            pl.BlockSpec((1, D_MODEL), const),
            pl.BlockSpec(w_in.shape, const),
            pl.BlockSpec(w_out.shape, const),
            pl.BlockSpec((1, D_MODEL), const),
        ],
        out_specs=rest_tok(D_MODEL),
        out_shape=jax.ShapeDtypeStruct((bsz, t, D_MODEL), F32),
        scratch_shapes=[pltpu.VMEM((HEAD_V, D_KEY), F32), pltpu.VMEM((tm, D_VAL), BF16)],
        compiler_params=pltpu.CompilerParams(
            dimension_semantics=("arbitrary",), vmem_limit_bytes=VMEM_LIMIT),
        name="fwd_ffn",
    )(qf, kf, v, ob, gs, fst, s0, g_head, x, sga, mb, mod, w_go, w_mo, g_ffn, w_in, w_out, g_final)


def kernel(x, c, ctx, c_ctx, w_ada, b_ada, g_mix, w_in, w_dec_up, b_dec, g_head, w_gla_out,
           w_conv, w_conv_out, w_mix_out, g_ffn, w_ffn_in, w_ffn_out, g_final):
    assert w_ada.shape[0] == 1, "single-layer block"
    bsz = x.shape[0]
    assert x.shape[1] % TM_REV == 0 and x.shape[1] % TM_FWD == 0
    assert ctx.shape[1] % CHUNK == 0 and sum(FF_GROUPS) == D_FF

    assert w_in.shape[2] == N_IN
    w_pack = _pack_call(jnp.transpose(w_in[0]))
    w_dec = jnp.zeros((LANES, 2 * D_KEY), F32)
    w_dec = w_dec.at[:DECAY_RANK, :D_KEY].set(w_dec_up[0, 0])
    w_dec = w_dec.at[DECAY_RANK:2 * DECAY_RANK, D_KEY:].set(w_dec_up[0, 1]).astype(BF16)
    b_dec2 = b_dec[0].reshape(1, 2 * D_KEY)

    n_rows = -(-(bsz + 1) // 8) * 8
    c_rows = jnp.zeros((n_rows, D_MODEL), F32).at[:bsz].set(c).at[bsz].set(c_ctx)
    mod = _ada_call(c_rows, w_ada[0], b_ada).reshape(n_rows, 6, D_MODEL)

    g_mix2 = g_mix.reshape(1, D_MODEL)
    s_cf, s_cb = _ctx_call(ctx, mod, g_mix2, w_pack, w_dec, b_dec2, bsz)
    qf, kf, v, ob, gs, sga, mb, fst = _proj_bwd_call(
        x, mod, g_mix2, w_pack, w_dec, b_dec2, w_conv[0], w_conv_out[0].astype(BF16), s_cb)
    return _fwd_ffn_call(x, mod, qf, kf, v, ob, gs, sga, mb, fst, g_head.reshape(1, HEAD_V),
                         w_gla_out[0].astype(BF16), w_mix_out[0].astype(BF16), s_cf,
                         g_ffn.reshape(1, D_MODEL), w_ffn_in[0].astype(BF16),
                         w_ffn_out[0].astype(BF16), g_final.reshape(1, D_MODEL))
```

```python
import functools

import jax
import jax.numpy as jnp
from jax import lax
from jax.experimental import pallas as pl
from jax.experimental.pallas import tpu as pltpu

D_MODEL = 1024
N_HEADS = 4
D_KEY = 512
D_VAL = 1024
HEAD_K = D_KEY // N_HEADS
HEAD_V = D_VAL // N_HEADS
DECAY_RANK = 16
DECAY_TAU = 16.0
CHUNK = 64
GRID_W = 64
D_FF = 2816
EPS = 1e-6

LANES = 128
MXU_TILE = 256
TM_REV = 512
TM_FWD = 512
PROJ_PARTS = 4
COL_Q, COL_K, COL_V, COL_G = 0, D_KEY, 2 * D_KEY, 2 * D_KEY + D_VAL
COL_DD = COL_G + D_VAL
COL_CB = COL_DD + D_MODEL
COL_CC, COL_CH, COL_GA, COL_GB = (COL_CB + D_MODEL, COL_CB + 2 * D_MODEL, COL_CB + 3 * D_MODEL,
                                  COL_CB + 4 * D_MODEL)
N_PACK = COL_GB + D_MODEL
N_IN = N_PACK - (COL_CB - COL_DD) + 2 * DECAY_RANK
PACK_COLS = 1024
FF_GROUPS = (512, 512, 512, 512, 512, 256)
VMEM_LIMIT = 60 * 1024 * 1024

BF16 = jnp.bfloat16
F32 = jnp.float32

NT_DIMS = (((1,), (1,)), ((), ()))
TN_DIMS = (((0,), (0,)), ((), ()))


def _dot(a, b):
    return jnp.dot(a, b, preferred_element_type=F32)


def _sigmoid(x):
    return 0.5 * jnp.tanh(0.5 * x) + 0.5


def _silu(x):
    return x * _sigmoid(x)


def _log_sigmoid(x):
    return jnp.minimum(x, 0.0) - jnp.log(1.0 + jnp.exp(-jnp.abs(x)))


def _rms(x):
    return x * lax.rsqrt(jnp.mean(x * x, axis=-1, keepdims=True) + EPS)


def _norm_modulate(x, g, shift, scale):
    return (_rms(x) * (g * (1.0 + scale)) + shift).astype(BF16)


def _cumsum_matrix(n, upper):
    r = lax.broadcasted_iota(jnp.int32, (n, n), 0)
    c = lax.broadcasted_iota(jnp.int32, (n, n), 1)
    same = (r // CHUNK) == (c // CHUNK)
    tri = (c >= r) if upper else (c <= r)
    return jnp.where(same & tri, 1.0, 0.0).astype(BF16)


def _chunk_cumsum(la, upper):
    rows = min(la.shape[0], MXU_TILE)
    mat = _cumsum_matrix(rows, upper)
    hi = la.astype(BF16)
    lo = (la - hi.astype(F32)).astype(BF16)
    parts = [_dot(mat, hi[r:r + rows]) + _dot(mat, lo[r:r + rows])
             for r in range(0, la.shape[0], rows)]
    return parts[0] if len(parts) == 1 else jnp.concatenate(parts, axis=0)


def _log_decay(hb, wdd_ref, wdec_ref, bdec_ref):
    dd = _dot(hb, wdd_ref[...]).astype(BF16)
    zd = _dot(dd, wdec_ref[...]) + bdec_ref[...]
    return _log_sigmoid(zd) * (1.0 / DECAY_TAU)


def _kv_all_heads(kc, vc):
    v_stack = jnp.concatenate([vc[:, h * HEAD_V:(h + 1) * HEAD_V] for h in range(N_HEADS)], axis=0)
    r = lax.broadcasted_iota(jnp.int32, (N_HEADS * CHUNK, D_KEY), 0) // CHUNK
    c = lax.broadcasted_iota(jnp.int32, (N_HEADS * CHUNK, D_KEY), 1) // HEAD_K
    k_blk = jnp.where(r == c, jnp.concatenate([kc] * N_HEADS, axis=0), jnp.zeros((), kc.dtype))
    return lax.dot_general(v_stack, k_blk, TN_DIMS, preferred_element_type=F32)


def _make_fill(items, n_gaps):
    queue = list(reversed(items))
    total = len(items)
    calls = [0]

    def fill():
        calls[0] += 1
        due = min(total, -(-calls[0] * total // n_gaps))
        while total - len(queue) < due:
            queue.pop()()

    def drain():
        while queue:
            queue.pop()()

    return fill, drain


def _no_fill():
    pass


def _gla_chunk(qc, kc, vc, st_ref, e_mid, e_last, e_gap, upper, fill=_no_fill):
    r = lax.broadcasted_iota(jnp.int32, (CHUNK, CHUNK), 0)
    c = lax.broadcasted_iota(jnp.int32, (CHUNK, CHUNK), 1)
    keep = (c >= r) if upper else (c <= r)
    heads = [(slice(h * HEAD_K, (h + 1) * HEAD_K), slice(h * HEAD_V, (h + 1) * HEAD_V))
             for h in range(N_HEADS)]
    scores = [lax.dot_general(qc[:, ks], kc[:, ks], NT_DIMS, preferred_element_type=F32)
              for ks, _ in heads]
    fill()
    st = st_ref[...]
    st_in = (st * e_mid).astype(BF16)
    outs = []
    for (ks, vs), s in zip(heads, scores):
        a = jnp.where(keep, s, 0.0).astype(BF16)
        outs.append(_dot(a, vc[:, vs])
                    + lax.dot_general(qc[:, ks], st_in[:, ks], NT_DIMS, preferred_element_type=F32))
    fill()
    st_ref[...] = st * e_last + _kv_all_heads(kc, vc) * e_gap
    fill()
    return jnp.concatenate(outs, axis=-1)


def _state_update_only(kc, vc, st_ref, e_last, e_gap):
    st_ref[...] = st_ref[...] * e_last + _kv_all_heads(kc, vc) * e_gap


def _pack_kernel(a_ref, b_ref, o_ref):
    j = pl.program_id(0)
    dd_block = COL_DD // PACK_COLS
    first_shifted = COL_CB // PACK_COLS
    dd_rows = 2 * DECAY_RANK

    @pl.when(j < dd_block)
    def _():
        o_ref[...] = a_ref[...].T.astype(BF16)

    @pl.when(j == dd_block)
    def _():
        row = lax.broadcasted_iota(jnp.int32, a_ref.shape, 0)
        o_ref[...] = jnp.where(row < dd_rows, a_ref[...], 0.0).T.astype(BF16)

    @pl.when((j > dd_block) & (j < first_shifted))
    def _():
        o_ref[...] = jnp.zeros(o_ref.shape, BF16)

    @pl.when(j >= first_shifted)
    def _():
        src = jnp.concatenate([a_ref[dd_rows:, :], b_ref[...]], axis=0)
        o_ref[...] = src.T.astype(BF16)


def _pack_call(w_t):
    dd_block = COL_DD // PACK_COLS
    first_shifted = COL_CB // PACK_COLS
    dd_rows = 2 * DECAY_RANK
    a_idx = lambda j: jnp.where(j <= dd_block, j, jnp.maximum(j - (first_shifted - dd_block), dd_block))
    return pl.pallas_call(
        _pack_kernel,
        grid=(N_PACK // PACK_COLS,),
        in_specs=[
            pl.BlockSpec((PACK_COLS, D_MODEL), lambda j: (a_idx(j), 0)),
            pl.BlockSpec((dd_rows, D_MODEL), lambda j: ((a_idx(j) + 1) * (PACK_COLS // dd_rows), 0)),
        ],
        out_specs=pl.BlockSpec((D_MODEL, PACK_COLS), lambda j: (0, j)),
        out_shape=jax.ShapeDtypeStruct((D_MODEL, N_PACK), BF16),
        name="pack_w_in",
    )(w_t, w_t)


def _ada_kernel(c_ref, w_ref, b_ref, o_ref):
    s = _silu(c_ref[...]).astype(BF16)
    o_ref[...] = _dot(s, w_ref[...].astype(BF16)) + b_ref[...]


def _ada_call(c_rows, w_ada, b_ada):
    n_rows = c_rows.shape[0]
    n_out = w_ada.shape[1]
    bn = D_MODEL
    return pl.pallas_call(
        _ada_kernel,
        grid=(n_out // bn,),
        in_specs=[
            pl.BlockSpec((n_rows, D_MODEL), lambda j: (0, 0)),
            pl.BlockSpec((D_MODEL, bn), lambda j: (0, j)),
            pl.BlockSpec((1, bn), lambda j: (0, j)),
        ],
        out_specs=pl.BlockSpec((n_rows, bn), lambda j: (0, j)),
        out_shape=jax.ShapeDtypeStruct((n_rows, n_out), F32),
        name="ada",
    )(c_rows, w_ada, b_ada)


def _ctx_kernel(x_ref, mod_ref, g_ref, wk_ref, wv_ref, wdd_ref, wdec_ref, bdec_ref,
                sf_ref, sb_ref):
    n = x_ref.shape[1]
    nch = n // CHUNK
    hb = _norm_modulate(x_ref[0], g_ref[...], mod_ref[0, 0:1, :], mod_ref[0, 1:2, :])
    la = _log_decay(hb, wdd_ref, wdec_ref, bdec_ref)
    b_f = _chunk_cumsum(la[:, :D_KEY], False)
    b_b = _chunk_cumsum(la[:, D_KEY:], True)
    k = _dot(hb, wk_ref[...])
    v = _dot(hb, wv_ref[...]).astype(BF16)
    sf_ref[0] = jnp.zeros(sf_ref.shape[1:], F32)
    sb_ref[0] = jnp.zeros(sb_ref.shape[1:], F32)
    mid = CHUNK // 2
    for ci in range(nch):
        rows = slice(ci * CHUNK, (ci + 1) * CHUNK)
        bc = b_f[rows]
        b_mid, b_last = bc[mid - 1:mid], bc[CHUNK - 1:CHUNK]
        kc = (k[rows] * jnp.exp(b_mid - bc)).astype(BF16)
        _state_update_only(kc, v[rows], sf_ref.at[0], jnp.exp(b_last), jnp.exp(b_last - b_mid))
        rows = slice((nch - 1 - ci) * CHUNK, (nch - ci) * CHUNK)
        bc = b_b[rows]
        b_mid, b_last = bc[mid:mid + 1], bc[0:1]
        kc = (k[rows] * jnp.exp(b_mid - bc)).astype(BF16)
        _state_update_only(kc, v[rows], sb_ref.at[0], jnp.exp(b_last), jnp.exp(b_last - b_mid))


def _ctx_call(ctx, mod, g_mix, w_pack, w_dec, b_dec, ctx_row):
    bsz, n, _ = ctx.shape
    const = lambda b: (0, 0)
    w_cols = lambda width, start: pl.BlockSpec((D_MODEL, width), lambda b: (0, start // width),
                                               pipeline_mode=pl.Buffered(1))
    st_spec = pl.BlockSpec((1, HEAD_V, D_KEY), lambda b: (b, 0, 0))
    st_shape = jax.ShapeDtypeStruct((bsz, HEAD_V, D_KEY), F32)
    return pl.pallas_call(
        _ctx_kernel,
        grid=(bsz,),
        in_specs=[
            pl.BlockSpec((1, n, D_MODEL), lambda b: (b, 0, 0)),
            pl.BlockSpec((1, 6, D_MODEL), lambda b: (ctx_row, 0, 0)),
            pl.BlockSpec((1, D_MODEL), const),
            w_cols(D_KEY, COL_K),
            w_cols(D_VAL, COL_V),
            w_cols(LANES, COL_DD),
            pl.BlockSpec(w_dec.shape, const),
            pl.BlockSpec(b_dec.shape, const),
        ],
        out_specs=[st_spec, st_spec],
        out_shape=[st_shape, st_shape],
        compiler_params=pltpu.CompilerParams(vmem_limit_bytes=VMEM_LIMIT),
        name="ctx_states",
    )(ctx, mod, g_mix, w_pack, w_pack, w_pack, w_dec, b_dec)


def _proj_bwd_kernel(nt, x0_ref, mod0_ref, xn_ref, modn_ref, g_ref, wqk_ref, wv_ref, wg_ref, wdd_ref,
                     wdec_ref, bdec_ref, wcb_ref, wcc_ref, wch_ref, wga_ref, wgb_ref, wconv_ref,
                     wco_ref, s0_ref,
                     qf_ref, kf_ref, v_ref, ob_ref, gs_ref, sga_ref, mb_ref, fst_ref,
                     st_ref, p_ref, hb_ref):
    tm = xn_ref.shape[1]
    nch = tm // CHUNK
    mid = CHUNK // 2
    s = pl.program_id(0)
    cur = s % 2

    @pl.when(s == 0)
    def _():
        hb_ref[0] = _norm_modulate(x0_ref[0], g_ref[...], mod0_ref[0, 0:1, :], mod0_ref[0, 1:2, :])

    @pl.when(s % nt == 0)
    def _():
        st_ref[...] = s0_ref[0]

    hb = hb_ref.at[cur]
    width = D_MODEL // PROJ_PARTS
    col_groups = tuple(slice(j * width, (j + 1) * width) for j in range(PROJ_PARTS))
    pos = lax.broadcasted_iota(jnp.int32, (tm, 1), 0) % GRID_W
    tmp = {}

    def proj(w_ref, cols):
        return _dot(hb[...], w_ref[:, cols])

    def gate_item(cols):
        def run():
            g = proj(wg_ref, cols)
            gs_ref[0, :, cols] = _silu(g).astype(BF16)
        return run

    def conv_in_item(name, w_ref, cols):
        def run():
            tmp[name] = proj(w_ref, cols)
        return run

    def conv_item(cols):
        def run():
            cb = proj(wcb_ref, cols)
            u = tmp.pop("cc") * tmp.pop("ch")
            u_prev = jnp.where(pos == 0, 0.0, pltpu.roll(u, 1, 0))
            u_next = jnp.where(pos == GRID_W - 1, 0.0, pltpu.roll(u, tm - 1, 0))
            uc = (u_prev * wconv_ref[0:1, cols] + u * wconv_ref[1:2, cols]
                  + u_next * wconv_ref[2:3, cols])
            p_ref[:, cols] = (cb * uc).astype(BF16)
        return run

    def conv_out_item(cols):
        def run():
            tmp["yc"] = _dot(p_ref[...], wco_ref[:, cols])
        return run

    def merge_gate_item(cols):
        def run():
            gb = proj(wgb_ref, cols)
            mb_ref[0, :, cols] = (_sigmoid(gb) * tmp.pop("yc")).astype(BF16)
        return run

    def gla_gate_item(cols):
        def run():
            ga = proj(wga_ref, cols)
            sga_ref[0, :, cols] = _sigmoid(ga).astype(BF16)
        return run

    def next_input_item():
        hb_ref[1 - cur] = _norm_modulate(xn_ref[0], g_ref[...], modn_ref[0, 0:1, :],
                                         modn_ref[0, 1:2, :])

    items = [gate_item(c) for c in col_groups]
    for c in col_groups:
        items += [conv_in_item("cc", wcc_ref, c), conv_in_item("ch", wch_ref, c), conv_item(c)]
    items += [next_input_item]
    for c in col_groups:
        items += [conv_out_item(c), merge_gate_item(c)]
    items += [gla_gate_item(c) for c in col_groups]
    fill, drain = _make_fill(items, len(items))

    la = _log_decay(hb[...], wdd_ref, wdec_ref, bdec_ref)
    fill()
    b_f = _chunk_cumsum(la[:, :D_KEY], False)
    b_b = _chunk_cumsum(la[:, D_KEY:], True)
    fill()

    q = proj(wqk_ref, slice(0, D_KEY)) * (HEAD_K ** -0.5)
    k = proj(wqk_ref, slice(D_KEY, 2 * D_KEY))
    vb = proj(wv_ref, slice(0, D_VAL)).astype(BF16)
    v_ref[0] = vb

    for ci in reversed(range(nch)):
        rows = slice(ci * CHUNK, (ci + 1) * CHUNK)
        bc = b_f[rows]
        b_mid, b_last = bc[mid - 1:mid], bc[CHUNK - 1:CHUNK]
        d = bc - b_mid
        qf_ref[0, rows, :] = (q[rows] * jnp.exp(d)).astype(BF16)
        kf_ref[0, rows, :] = (k[rows] * jnp.exp(-d)).astype(BF16)
        fst_ref[0, 0, ci:ci + 1, :] = jnp.concatenate([b_mid, b_last], axis=-1)
        bc = b_b[rows]
        b_mid, b_last = bc[mid:mid + 1], bc[0:1]
        d = bc - b_mid
        qc = (q[rows] * jnp.exp(d)).astype(BF16)
        kc = (k[rows] * jnp.exp(-d)).astype(BF16)
        ob_ref[0, rows, :] = _gla_chunk(qc, kc, vb[rows], st_ref, jnp.exp(b_mid),
                                        jnp.exp(b_last), jnp.exp(b_last - b_mid), True, fill)
    drain()


def _proj_bwd_call(x, mod, g_mix, w_pack, w_dec, b_dec, w_conv, w_co, s0):
    bsz, t, _ = x.shape
    tm = TM_REV
    nt = t // tm
    nch = tm // CHUNK
    steps = bsz * nt
    cur = lambda s: (s // nt, nt - 1 - s % nt, 0)
    nxt = lambda s: cur(jnp.minimum(s + 1, steps - 1))
    const = lambda s: (0, 0)
    w_cols = lambda width, start: pl.BlockSpec((D_MODEL, width), lambda s: (0, start // width),
                                               pipeline_mode=pl.Buffered(1))
    tok = lambda width: pl.BlockSpec((1, tm, width), cur)
    tok_shape = lambda width, dt: jax.ShapeDtypeStruct((bsz, t, width), dt)
    return pl.pallas_call(
        functools.partial(_proj_bwd_kernel, nt),
        grid=(steps,),
        in_specs=[
            pl.BlockSpec((1, tm, D_MODEL), lambda s: (0, nt - 1, 0)),
            pl.BlockSpec((1, 6, D_MODEL), lambda s: (0, 0, 0)),
            pl.BlockSpec((1, tm, D_MODEL), nxt),
            pl.BlockSpec((1, 6, D_MODEL), lambda s: (jnp.minimum(s + 1, steps - 1) // nt, 0, 0)),
            pl.BlockSpec((1, D_MODEL), const),
            w_cols(2 * D_KEY, COL_Q), w_cols(D_VAL, COL_V), w_cols(D_VAL, COL_G), w_cols(LANES, COL_DD),
            pl.BlockSpec(w_dec.shape, const),
            pl.BlockSpec(b_dec.shape, const),
            w_cols(D_MODEL, COL_CB), w_cols(D_MODEL, COL_CC), w_cols(D_MODEL, COL_CH),
            w_cols(D_MODEL, COL_GA), w_cols(D_MODEL, COL_GB),
            pl.BlockSpec(w_conv.shape, const),
            pl.BlockSpec(w_co.shape, const),
            pl.BlockSpec((1, HEAD_V, D_KEY), lambda s: (s // nt, 0, 0)),
        ],
        out_specs=[
            tok(D_KEY), tok(D_KEY), tok(D_VAL), tok(D_VAL), tok(D_VAL), tok(D_MODEL), tok(D_MODEL),
            pl.BlockSpec((1, 1, nch, 2 * D_KEY), lambda s: (s // nt, nt - 1 - s % nt, 0, 0)),
        ],
        out_shape=[
            tok_shape(D_KEY, BF16), tok_shape(D_KEY, BF16), tok_shape(D_VAL, BF16),
            tok_shape(D_VAL, F32), tok_shape(D_VAL, BF16), tok_shape(D_MODEL, BF16),
            tok_shape(D_MODEL, BF16),
            jax.ShapeDtypeStruct((bsz, nt, nch, 2 * D_KEY), F32),
        ],
        scratch_shapes=[pltpu.VMEM((HEAD_V, D_KEY), F32), pltpu.VMEM((tm, D_MODEL), BF16),
                        pltpu.VMEM((2, tm, D_MODEL), BF16)],
        compiler_params=pltpu.CompilerParams(
            dimension_semantics=("arbitrary",), vmem_limit_bytes=VMEM_LIMIT),
        name="proj_bwd",
    )(x, mod, x, mod, g_mix, w_pack, w_pack, w_pack, w_pack, w_dec, b_dec,
      w_pack, w_pack, w_pack, w_pack, w_pack, w_conv, w_co, s0)


def _fwd_ffn_kernel(nt, steps, qf_ref, kf_ref, v_ref, ob_ref, gs_ref, fst_ref, s0_ref, gh_ref,
                    x_ref, sga_ref, mb_ref, mod_ref, wgo_ref, wmo_ref, gffn_ref, win_ref, wout_ref,
                    gfin_ref, o_ref, st_ref, a_ref):
    tm = x_ref.shape[1]
    nch = tm // CHUNK
    s = pl.program_id(0)
    t = jnp.minimum(s, steps - 1)

    @pl.when(s == 0)
    def _():
        a_ref[...] = jnp.zeros(a_ref.shape, a_ref.dtype)

    @pl.when(t % nt == 0)
    def _():
        st_ref[...] = s0_ref[0]

    tmp = {}

    def mix_item():
        merged = sga_ref[0].astype(F32) * tmp.pop("y_gla") + mb_ref[0].astype(F32)
        mix = _dot(merged.astype(BF16), wmo_ref[...])
        x1 = x_ref[0] + mod_ref[0, 2:3, :] * mix
        tmp["x1"] = x1
        tmp["hb"] = _norm_modulate(x1, gffn_ref[...], mod_ref[0, 3:4, :], mod_ref[0, 4:5, :])

    def ffn_in_item(name, cols):
        def run():
            tmp[name] = _dot(tmp["hb"], win_ref[:, cols])
        return run

    def ffn_out_item(cols):
        def run():
            hidden = (_silu(tmp.pop("a")) * tmp.pop("b")).astype(BF16)
            part = _dot(hidden, wout_ref[cols, :])
            tmp["y"] = part if "y" not in tmp else tmp["y"] + part
        return run

    items = [mix_item]
    start = 0
    for width in FF_GROUPS:
        cols = slice(start, start + width)
        gate_cols = slice(D_FF + start, D_FF + start + width)
        items += [ffn_in_item("a", cols), ffn_in_item("b", gate_cols), ffn_out_item(cols)]
        start += width
    fill, drain = _make_fill(items, 3 * nch)

    tmp["y_gla"] = _dot(a_ref[...], wgo_ref[...])

    for ci in range(nch):
        rows = slice(ci * CHUNK, (ci + 1) * CHUNK)
        b_mid = fst_ref[0, 0, ci:ci + 1, 0:D_KEY]
        b_last = fst_ref[0, 0, ci:ci + 1, D_KEY:]
        o = _gla_chunk(qf_ref[0, rows, :], kf_ref[0, rows, :], v_ref[0, rows, :], st_ref,
                       jnp.exp(b_mid), jnp.exp(b_last), jnp.exp(b_last - b_mid), False, fill)
        o = o + ob_ref[0, rows, :]
        for h in range(N_HEADS):
            vs = slice(h * HEAD_V, (h + 1) * HEAD_V)
            a_ref[rows, vs] = (_rms(o[:, vs]) * gh_ref[...]
                               * gs_ref[0, rows, vs].astype(F32)).astype(BF16)
    drain()

    x2 = tmp["x1"] + mod_ref[0, 5:6, :] * tmp["y"]
    o_ref[0] = _rms(x2) * gfin_ref[...]


def _fwd_ffn_call(x, mod, qf, kf, v, ob, gs, sga, mb, fst, g_head, w_go, w_mo, s0, g_ffn, w_in,
                  w_out, g_final):
    bsz, t, _ = x.shape
    tm = TM_FWD
    nt = t // tm
    nch = tm // CHUNK
    steps = bsz * nt
    fst = fst.reshape(bsz, nt, nch, 2 * D_KEY)
    scan = lambda s: (jnp.minimum(s, steps - 1) // nt, jnp.minimum(s, steps - 1) % nt, 0)
    rest = lambda s: (jnp.maximum(s - 1, 0) // nt, jnp.maximum(s - 1, 0) % nt, 0)
    const = lambda s: (0, 0)
    scan_tok = lambda width: pl.BlockSpec((1, tm, width), scan)
    rest_tok = lambda width: pl.BlockSpec((1, tm, width), rest)
    return pl.pallas_call(
        functools.partial(_fwd_ffn_kernel, nt, steps),
        grid=(steps + 1,),
        in_specs=[
            scan_tok(D_KEY), scan_tok(D_KEY), scan_tok(D_VAL), scan_tok(D_VAL), scan_tok(D_VAL),
            pl.BlockSpec((1, 1, nch, 2 * D_KEY), lambda s: scan(s) + (0,)),
            pl.BlockSpec((1, HEAD_V, D_KEY), lambda s: (jnp.minimum(s, steps - 1) // nt, 0, 0)),
            pl.BlockSpec((1, HEAD_V), const),
            rest_tok(D_MODEL), rest_tok(D_MODEL), rest_tok(D_MODEL),
            pl.BlockSpec((1, 6, D_MODEL), lambda s: (jnp.maximum(s - 1, 0) // nt, 0, 0)),
            pl.BlockSpec(w_go.shape, const),
            pl.BlockSpec(w_mo.shape, const),
            pl.BlockSpec((1, D_MODEL), const),
            pl.BlockSpec(w_in.shape, const),
            pl.BlockSpec(w_out.shape, const),
            pl.BlockSpec((1, D_MODEL), const),
        ],
        out_specs=rest_tok(D_MODEL),
        out_shape=jax.ShapeDtypeStruct((bsz, t, D_MODEL), F32),
        scratch_shapes=[pltpu.VMEM((HEAD_V, D_KEY), F32), pltpu.VMEM((tm, D_VAL), BF16)],
        compiler_params=pltpu.CompilerParams(
            dimension_semantics=("arbitrary",), vmem_limit_bytes=VMEM_LIMIT),
        name="fwd_ffn",
    )(qf, kf, v, ob, gs, fst, s0, g_head, x, sga, mb, mod, w_go, w_mo, g_ffn, w_in, w_out, g_final)


def kernel(x, c, ctx, c_ctx, w_ada, b_ada, g_mix, w_in, w_dec_up, b_dec, g_head, w_gla_out,
           w_conv, w_conv_out, w_mix_out, g_ffn, w_ffn_in, w_ffn_out, g_final):
    assert w_ada.shape[0] == 1, "single-layer block"
    bsz = x.shape[0]
    assert x.shape[1] % TM_REV == 0 and x.shape[1] % TM_FWD == 0
    assert ctx.shape[1] % CHUNK == 0 and sum(FF_GROUPS) == D_FF

    assert w_in.shape[2] == N_IN
    w_pack = _pack_call(jnp.transpose(w_in[0]))
    w_dec = jnp.zeros((LANES, 2 * D_KEY), F32)
    w_dec = w_dec.at[:DECAY_RANK, :D_KEY].set(w_dec_up[0, 0])
    w_dec = w_dec.at[DECAY_RANK:2 * DECAY_RANK, D_KEY:].set(w_dec_up[0, 1]).astype(BF16)
    b_dec2 = b_dec[0].reshape(1, 2 * D_KEY)

    n_rows = -(-(bsz + 1) // 8) * 8
    c_rows = jnp.zeros((n_rows, D_MODEL), F32).at[:bsz].set(c).at[bsz].set(c_ctx)
    mod = _ada_call(c_rows, w_ada[0], b_ada).reshape(n_rows, 6, D_MODEL)

    g_mix2 = g_mix.reshape(1, D_MODEL)
    s_cf, s_cb = _ctx_call(ctx, mod, g_mix2, w_pack, w_dec, b_dec2, bsz)
    qf, kf, v, ob, gs, sga, mb, fst = _proj_bwd_call(
        x, mod, g_mix2, w_pack, w_dec, b_dec2, w_conv[0], w_conv_out[0].astype(BF16), s_cb)
    return _fwd_ffn_call(x, mod, qf, kf, v, ob, gs, sga, mb, fst, g_head.reshape(1, HEAD_V),
                         w_gla_out[0].astype(BF16), w_mix_out[0].astype(BF16), s_cf,
                         g_ffn.reshape(1, D_MODEL), w_ffn_in[0].astype(BF16),
                         w_ffn_out[0].astype(BF16), g_final.reshape(1, D_MODEL))
```

```python
import functools

import jax
import jax.numpy as jnp
from jax import lax
from jax.experimental import pallas as pl
from jax.experimental.pallas import tpu as pltpu

D_MODEL = 1024
N_HEADS = 4
D_KEY = 512
D_VAL = 1024
HEAD_K = D_KEY // N_HEADS
HEAD_V = D_VAL // N_HEADS
DECAY_RANK = 16
DECAY_TAU = 16.0
CHUNK = 64
GRID_W = 64
D_FF = 2816
EPS = 1e-6

LANES = 128
BF16_ROWS = 16
MXU_TILE = 256
TM_REV = 512
TM_FWD = 512
PROJ_PARTS = 4
COL_Q, COL_K, COL_V, COL_G = 0, D_KEY, 2 * D_KEY, 2 * D_KEY + D_VAL
COL_DD = COL_G + D_VAL
COL_CB = COL_DD + D_MODEL
COL_CC, COL_CH, COL_GA, COL_GB = (COL_CB + D_MODEL, COL_CB + 2 * D_MODEL, COL_CB + 3 * D_MODEL,
                                  COL_CB + 4 * D_MODEL)
N_PACK = COL_GB + D_MODEL
N_IN = N_PACK - (COL_CB - COL_DD) + 2 * DECAY_RANK
PACK_COLS = 1024
FF_GROUPS = (512, 512, 512, 512, 512, 256)
VMEM_LIMIT = 60 * 1024 * 1024

BF16 = jnp.bfloat16
F32 = jnp.float32

NT_DIMS = (((1,), (1,)), ((), ()))
TN_DIMS = (((0,), (0,)), ((), ()))


def _dot(a, b):
    return jnp.dot(a, b, preferred_element_type=F32)


def _sigmoid(x):
    return 0.5 * jnp.tanh(0.5 * x) + 0.5


def _silu(x):
    return x * _sigmoid(x)


def _log_sigmoid(x):
    return jnp.minimum(x, 0.0) - jnp.log(1.0 + jnp.exp(-jnp.abs(x)))


def _rms(x):
    return x * lax.rsqrt(jnp.mean(x * x, axis=-1, keepdims=True) + EPS)


def _norm_modulate(x, g, shift, scale):
    return (_rms(x) * (g * (1.0 + scale)) + shift).astype(BF16)


def _cumsum_matrix(n, upper):
    r = lax.broadcasted_iota(jnp.int32, (n, n), 0)
    c = lax.broadcasted_iota(jnp.int32, (n, n), 1)
    same = (r // CHUNK) == (c // CHUNK)
    tri = (c >= r) if upper else (c <= r)
    return jnp.where(same & tri, 1.0, 0.0).astype(BF16)


def _chunk_cumsum(la, upper):
    rows = min(la.shape[0], MXU_TILE)
    mat = _cumsum_matrix(rows, upper)
    hi = la.astype(BF16)
    lo = (la - hi.astype(F32)).astype(BF16)
    parts = [_dot(mat, hi[r:r + rows]) + _dot(mat, lo[r:r + rows])
             for r in range(0, la.shape[0], rows)]
    return parts[0] if len(parts) == 1 else jnp.concatenate(parts, axis=0)


def _log_decay(hb, wdd_ref, wdec_ref, bdec_ref):
    dd = _dot(hb, wdd_ref[...]).astype(BF16)
    zd = _dot(dd, wdec_ref[...]) + bdec_ref[...]
    return _log_sigmoid(zd) * (1.0 / DECAY_TAU)


def _kv_all_heads(kc, vc):
    v_stack = jnp.concatenate([vc[:, h * HEAD_V:(h + 1) * HEAD_V] for h in range(N_HEADS)], axis=0)
    r = lax.broadcasted_iota(jnp.int32, (N_HEADS * CHUNK, D_KEY), 0) // CHUNK
    c = lax.broadcasted_iota(jnp.int32, (N_HEADS * CHUNK, D_KEY), 1) // HEAD_K
    k_blk = jnp.where(r == c, jnp.concatenate([kc] * N_HEADS, axis=0), jnp.zeros((), kc.dtype))
    return lax.dot_general(v_stack, k_blk, TN_DIMS, preferred_element_type=F32)


def _make_fill(items, n_gaps):
    queue = list(reversed(items))
    total = len(items)
    calls = [0]

    def fill():
        calls[0] += 1
        due = min(total, -(-calls[0] * total // n_gaps))
        while total - len(queue) < due:
            queue.pop()()

    def drain():
        while queue:
            queue.pop()()

    return fill, drain


def _no_fill():
    pass


def _gla_chunk(qc, kc, vc, st_ref, e_mid, e_last, e_gap, upper, fill=_no_fill):
    r = lax.broadcasted_iota(jnp.int32, (CHUNK, CHUNK), 0)
    c = lax.broadcasted_iota(jnp.int32, (CHUNK, CHUNK), 1)
    keep = (c >= r) if upper else (c <= r)
    heads = [(slice(h * HEAD_K, (h + 1) * HEAD_K), slice(h * HEAD_V, (h + 1) * HEAD_V))
             for h in range(N_HEADS)]
    scores = [lax.dot_general(qc[:, ks], kc[:, ks], NT_DIMS, preferred_element_type=F32)
              for ks, _ in heads]
    fill()
    st = st_ref[...]
    st_in = (st * e_mid).astype(BF16)
    outs = []
    for (ks, vs), s in zip(heads, scores):
        a = jnp.where(keep, s, 0.0).astype(BF16)
        outs.append(_dot(a, vc[:, vs])
                    + lax.dot_general(qc[:, ks], st_in[:, ks], NT_DIMS, preferred_element_type=F32))
    fill()
    st_ref[...] = st * e_last + _kv_all_heads(kc, vc) * e_gap
    fill()
    return jnp.concatenate(outs, axis=-1)


def _state_update_only(kc, vc, st_ref, e_last, e_gap):
    st_ref[...] = st_ref[...] * e_last + _kv_all_heads(kc, vc) * e_gap


def _pack_kernel(a_ref, b_ref, o_ref):
    j = pl.program_id(0)
    dd_block = COL_DD // PACK_COLS
    first_shifted = COL_CB // PACK_COLS
    dd_rows = 2 * DECAY_RANK

    @pl.when(j < dd_block)
    def _():
        o_ref[...] = a_ref[...].T.astype(BF16)

    @pl.when(j == dd_block)
    def _():
        row = lax.broadcasted_iota(jnp.int32, a_ref.shape, 0)
        o_ref[...] = jnp.where(row < dd_rows, a_ref[...], 0.0).T.astype(BF16)

    @pl.when((j > dd_block) & (j < first_shifted))
    def _():
        o_ref[...] = jnp.zeros(o_ref.shape, BF16)

    @pl.when(j >= first_shifted)
    def _():
        src = jnp.concatenate([a_ref[dd_rows:, :], b_ref[...]], axis=0)
        o_ref[...] = src.T.astype(BF16)


def _pack_call(w_t):
    dd_block = COL_DD // PACK_COLS
    first_shifted = COL_CB // PACK_COLS
    dd_rows = 2 * DECAY_RANK
    a_idx = lambda j: jnp.where(j <= dd_block, j, jnp.maximum(j - (first_shifted - dd_block), dd_block))
    return pl.pallas_call(
        _pack_kernel,
        grid=(N_PACK // PACK_COLS,),
        in_specs=[
            pl.BlockSpec((PACK_COLS, D_MODEL), lambda j: (a_idx(j), 0)),
            pl.BlockSpec((dd_rows, D_MODEL), lambda j: ((a_idx(j) + 1) * (PACK_COLS // dd_rows), 0)),
        ],
        out_specs=pl.BlockSpec((D_MODEL, PACK_COLS), lambda j: (0, j)),
        out_shape=jax.ShapeDtypeStruct((D_MODEL, N_PACK), BF16),
        name="pack_w_in",
    )(w_t, w_t)


def _ada_kernel(c_ref, w_ref, b_ref, o_ref):
    s = _silu(c_ref[...]).astype(BF16)
    o_ref[...] = _dot(s, w_ref[...].astype(BF16)) + b_ref[...]


def _ada_call(c_rows, w_ada, b_ada):
    n_rows = c_rows.shape[0]
    n_out = w_ada.shape[1]
    bn = D_MODEL
    return pl.pallas_call(
        _ada_kernel,
        grid=(n_out // bn,),
        in_specs=[
            pl.BlockSpec((n_rows, D_MODEL), lambda j: (0, 0)),
            pl.BlockSpec((D_MODEL, bn), lambda j: (0, j)),
            pl.BlockSpec((1, bn), lambda j: (0, j)),
        ],
        out_specs=pl.BlockSpec((n_rows, bn), lambda j: (0, j)),
        out_shape=jax.ShapeDtypeStruct((n_rows, n_out), F32),
        name="ada",
    )(c_rows, w_ada, b_ada)


def _ctx_kernel(x_ref, mod_ref, g_ref, wk_ref, wv_ref, wdd_ref, wdec_ref, bdec_ref,
                sf_ref, sb_ref):
    bsz, n, _ = x_ref.shape
    nch = n // CHUNK
    x = x_ref[...].reshape(bsz * n, D_MODEL)
    hb = _norm_modulate(x, g_ref[...], mod_ref[0, 0:1, :], mod_ref[0, 1:2, :])
    la = _log_decay(hb, wdd_ref, wdec_ref, bdec_ref)
    b_f = _chunk_cumsum(la[:, :D_KEY], False)
    b_b = _chunk_cumsum(la[:, D_KEY:], True)
    k = _dot(hb, wk_ref[...])
    v = _dot(hb, wv_ref[...]).astype(BF16)
    sf_ref[...] = jnp.zeros(sf_ref.shape, F32)
    sb_ref[...] = jnp.zeros(sb_ref.shape, F32)
    mid = CHUNK // 2
    for ci in range(nch):
        for b in range(bsz):
            rows = slice(b * n + ci * CHUNK, b * n + (ci + 1) * CHUNK)
            bc = b_f[rows]
            b_mid, b_last = bc[mid - 1:mid], bc[CHUNK - 1:CHUNK]
            kc = (k[rows] * jnp.exp(b_mid - bc)).astype(BF16)
            _state_update_only(kc, v[rows], sf_ref.at[b], jnp.exp(b_last), jnp.exp(b_last - b_mid))
            rows = slice(b * n + (nch - 1 - ci) * CHUNK, b * n + (nch - ci) * CHUNK)
            bc = b_b[rows]
            b_mid, b_last = bc[mid:mid + 1], bc[0:1]
            kc = (k[rows] * jnp.exp(b_mid - bc)).astype(BF16)
            _state_update_only(kc, v[rows], sb_ref.at[b], jnp.exp(b_last), jnp.exp(b_last - b_mid))


def _ctx_call(ctx, mod, g_mix, w_pack, w_dec, b_dec, ctx_row):
    bsz, n, _ = ctx.shape
    const = lambda b: (0, 0)
    w_cols = lambda width, start: pl.BlockSpec((D_MODEL, width), lambda b: (0, start // width),
                                               pipeline_mode=pl.Buffered(1))
    st_spec = pl.BlockSpec((bsz, HEAD_V, D_KEY), lambda b: (0, 0, 0))
    st_shape = jax.ShapeDtypeStruct((bsz, HEAD_V, D_KEY), F32)
    return pl.pallas_call(
        _ctx_kernel,
        grid=(1,),
        in_specs=[
            pl.BlockSpec((bsz, n, D_MODEL), lambda b: (0, 0, 0)),
            pl.BlockSpec((1, 6, D_MODEL), lambda b: (ctx_row, 0, 0)),
            pl.BlockSpec((1, D_MODEL), const),
            w_cols(D_KEY, COL_K),
            w_cols(D_VAL, COL_V),
            w_cols(LANES, COL_DD),
            pl.BlockSpec(w_dec.shape, const),
            pl.BlockSpec(b_dec.shape, const),
        ],
        out_specs=[st_spec, st_spec],
        out_shape=[st_shape, st_shape],
        compiler_params=pltpu.CompilerParams(vmem_limit_bytes=VMEM_LIMIT),
        name="ctx_states",
    )(ctx, mod, g_mix, w_pack, w_pack, w_pack, w_dec, b_dec)


def _proj_bwd_kernel(nt, x0_ref, mod0_ref, xn_ref, modn_ref, g_ref, wqk_ref, wv_ref, wg_ref, wdd_ref,
                     wdec_ref, bdec_ref, wcb_ref, wcc_ref, wch_ref, wga_ref, wgb_ref, wconv_ref,
                     wco_ref, s0_ref, later_f32_refs,
                     qf_ref, kf_ref, v_ref, ob_ref, gs_ref, sga_ref, mb_ref, fst_ref, later_bf16_refs,
                     st_ref, p_ref, hb_ref):
    tm = xn_ref.shape[1]
    nch = tm // CHUNK
    mid = CHUNK // 2
    s = pl.program_id(0)
    cur = s % 2

    @pl.when(s == 0)
    def _():
        hb_ref[0] = _norm_modulate(x0_ref[0], g_ref[...], mod0_ref[0, 0:1, :], mod0_ref[0, 1:2, :])

    @pl.when(s % nt == 0)
    def _():
        st_ref[...] = s0_ref[0]

    hb = hb_ref.at[cur]
    width = D_MODEL // PROJ_PARTS
    col_groups = tuple(slice(j * width, (j + 1) * width) for j in range(PROJ_PARTS))
    pos = lax.broadcasted_iota(jnp.int32, (tm, 1), 0) % GRID_W
    tmp = {}

    def proj(w_ref, cols):
        return _dot(hb[...], w_ref[:, cols])

    def gate_item(cols):
        def run():
            g = proj(wg_ref, cols)
            gs_ref[0, :, cols] = _silu(g).astype(BF16)
        return run

    def conv_in_item(name, w_ref, cols):
        def run():
            tmp[name] = proj(w_ref, cols)
        return run

    def conv_item(cols):
        def run():
            cb = proj(wcb_ref, cols)
            u = tmp.pop("cc") * tmp.pop("ch")
            u_prev = jnp.where(pos == 0, 0.0, pltpu.roll(u, 1, 0))
            u_next = jnp.where(pos == GRID_W - 1, 0.0, pltpu.roll(u, tm - 1, 0))
            uc = (u_prev * wconv_ref[0:1, cols] + u * wconv_ref[1:2, cols]
                  + u_next * wconv_ref[2:3, cols])
            p_ref[:, cols] = (cb * uc).astype(BF16)
        return run

    def conv_out_item(cols):
        def run():
            tmp["yc"] = _dot(p_ref[...], wco_ref[:, cols])
        return run

    def merge_gate_item(cols):
        def run():
            gb = proj(wgb_ref, cols)
            mb_ref[0, :, cols] = (_sigmoid(gb) * tmp.pop("yc")).astype(BF16)
        return run

    def gla_gate_item(cols):
        def run():
            ga = proj(wga_ref, cols)
            sga_ref[0, :, cols] = _sigmoid(ga).astype(BF16)
        return run

    def next_input_item():
        hb_ref[1 - cur] = _norm_modulate(xn_ref[0], g_ref[...], modn_ref[0, 0:1, :],
                                         modn_ref[0, 1:2, :])
        for src, dst in zip(later_f32_refs, later_bf16_refs):
            dst[...] = src[...].astype(BF16)

    items = [gate_item(c) for c in col_groups]
    for c in col_groups:
        items += [conv_in_item("cc", wcc_ref, c), conv_in_item("ch", wch_ref, c), conv_item(c)]
    items += [next_input_item]
    for c in col_groups:
        items += [conv_out_item(c), merge_gate_item(c)]
    items += [gla_gate_item(c) for c in col_groups]
    fill, drain = _make_fill(items, len(items))

    la = _log_decay(hb[...], wdd_ref, wdec_ref, bdec_ref)
    fill()
    b_f = _chunk_cumsum(la[:, :D_KEY], False)
    b_b = _chunk_cumsum(la[:, D_KEY:], True)
    fill()

    q = proj(wqk_ref, slice(0, D_KEY)) * (HEAD_K ** -0.5)
    k = proj(wqk_ref, slice(D_KEY, 2 * D_KEY))
    vb = proj(wv_ref, slice(0, D_VAL)).astype(BF16)
    v_ref[0] = vb

    for ci in reversed(range(nch)):
        rows = slice(ci * CHUNK, (ci + 1) * CHUNK)
        bc = b_f[rows]
        b_mid, b_last = bc[mid - 1:mid], bc[CHUNK - 1:CHUNK]
        d = bc - b_mid
        qf_ref[0, rows, :] = (q[rows] * jnp.exp(d)).astype(BF16)
        kf_ref[0, rows, :] = (k[rows] * jnp.exp(-d)).astype(BF16)
        fst_ref[0, 0, ci:ci + 1, :] = jnp.concatenate([b_mid, b_last], axis=-1)
        bc = b_b[rows]
        b_mid, b_last = bc[mid:mid + 1], bc[0:1]
        d = bc - b_mid
        qc = (q[rows] * jnp.exp(d)).astype(BF16)
        kc = (k[rows] * jnp.exp(-d)).astype(BF16)
        ob_ref[0, rows, :] = _gla_chunk(qc, kc, vb[rows], st_ref, jnp.exp(b_mid),
                                        jnp.exp(b_last), jnp.exp(b_last - b_mid), True, fill)
    drain()


N_PROJ_INPUTS = 19


def _proj_bwd_call(x, mod, g_mix, w_pack, w_dec, b_dec, w_conv, w_co, s0, later_weights):
    bsz, t, _ = x.shape
    tm = TM_REV
    nt = t // tm
    nch = tm // CHUNK
    steps = bsz * nt
    slab_specs = []
    for w in later_weights:
        rows = -(-w.shape[0] // (steps * BF16_ROWS)) * BF16_ROWS
        last = -(-w.shape[0] // rows) - 1
        slab_specs.append(pl.BlockSpec((rows, w.shape[1]), lambda s, last=last: (jnp.minimum(s, last), 0)))
    n_later = len(later_weights)
    cur = lambda s: (s // nt, nt - 1 - s % nt, 0)
    nxt = lambda s: cur(jnp.minimum(s + 1, steps - 1))
    const = lambda s: (0, 0)
    w_cols = lambda width, start: pl.BlockSpec((D_MODEL, width), lambda s: (0, start // width),
                                               pipeline_mode=pl.Buffered(1))
    tok = lambda width: pl.BlockSpec((1, tm, width), cur)
    tok_shape = lambda width, dt: jax.ShapeDtypeStruct((bsz, t, width), dt)
    def body(*refs):
        n_in = N_PROJ_INPUTS
        ins, rest = refs[:n_in], refs[n_in:]
        later_in, rest = rest[:n_later], rest[n_later:]
        outs, rest = rest[:8], rest[8:]
        later_out, scratch = rest[:n_later], rest[n_later:]
        _proj_bwd_kernel(nt, *ins, later_in, *outs, later_out, *scratch)

    res = pl.pallas_call(
        body,
        grid=(steps,),
        in_specs=[
            pl.BlockSpec((1, tm, D_MODEL), lambda s: (0, nt - 1, 0)),
            pl.BlockSpec((1, 6, D_MODEL), lambda s: (0, 0, 0)),
            pl.BlockSpec((1, tm, D_MODEL), nxt),
            pl.BlockSpec((1, 6, D_MODEL), lambda s: (jnp.minimum(s + 1, steps - 1) // nt, 0, 0)),
            pl.BlockSpec((1, D_MODEL), const),
            w_cols(2 * D_KEY, COL_Q), w_cols(D_VAL, COL_V), w_cols(D_VAL, COL_G), w_cols(LANES, COL_DD),
            pl.BlockSpec(w_dec.shape, const),
            pl.BlockSpec(b_dec.shape, const),
            w_cols(D_MODEL, COL_CB), w_cols(D_MODEL, COL_CC), w_cols(D_MODEL, COL_CH),
            w_cols(D_MODEL, COL_GA), w_cols(D_MODEL, COL_GB),
            pl.BlockSpec(w_conv.shape, const),
            pl.BlockSpec(w_co.shape, const),
            pl.BlockSpec((1, HEAD_V, D_KEY), lambda s: (s // nt, 0, 0)),
        ] + slab_specs,
        out_specs=[
            tok(D_KEY), tok(D_KEY), tok(D_VAL), tok(D_VAL), tok(D_VAL), tok(D_MODEL), tok(D_MODEL),
            pl.BlockSpec((1, 1, nch, 2 * D_KEY), lambda s: (s // nt, nt - 1 - s % nt, 0, 0)),
        ] + slab_specs,
        out_shape=[
            tok_shape(D_KEY, BF16), tok_shape(D_KEY, BF16), tok_shape(D_VAL, BF16),
            tok_shape(D_VAL, F32), tok_shape(D_VAL, BF16), tok_shape(D_MODEL, BF16),
            tok_shape(D_MODEL, BF16),
            jax.ShapeDtypeStruct((bsz, nt, nch, 2 * D_KEY), F32),
        ] + [jax.ShapeDtypeStruct(w.shape, BF16) for w in later_weights],
        scratch_shapes=[pltpu.VMEM((HEAD_V, D_KEY), F32), pltpu.VMEM((tm, D_MODEL), BF16),
                        pltpu.VMEM((2, tm, D_MODEL), BF16)],
        compiler_params=pltpu.CompilerParams(
            dimension_semantics=("arbitrary",), vmem_limit_bytes=VMEM_LIMIT),
        name="proj_bwd",
    )(x, mod, x, mod, g_mix, w_pack, w_pack, w_pack, w_pack, w_dec, b_dec,
      w_pack, w_pack, w_pack, w_pack, w_pack, w_conv, w_co, s0, *later_weights)
    return res[:8], res[8:]


def _fwd_ffn_kernel(nt, steps, qf_ref, kf_ref, v_ref, ob_ref, gs_ref, fst_ref, s0_ref, gh_ref,
                    x_ref, sga_ref, mb_ref, mod_ref, wgo_ref, wmo_ref, gffn_ref, win_ref, wout_ref,
                    gfin_ref, o_ref, st_ref, a_ref):
    tm = x_ref.shape[1]
    nch = tm // CHUNK
    s = pl.program_id(0)
    t = jnp.minimum(s, steps - 1)

    @pl.when(s == 0)
    def _():
        a_ref[...] = jnp.zeros(a_ref.shape, a_ref.dtype)

    @pl.when(t % nt == 0)
    def _():
        st_ref[...] = s0_ref[0]

    tmp = {}

    def mix_item():
        merged = sga_ref[0].astype(F32) * tmp.pop("y_gla") + mb_ref[0].astype(F32)
        mix = _dot(merged.astype(BF16), wmo_ref[...])
        x1 = x_ref[0] + mod_ref[0, 2:3, :] * mix
        tmp["x1"] = x1
        tmp["hb"] = _norm_modulate(x1, gffn_ref[...], mod_ref[0, 3:4, :], mod_ref[0, 4:5, :])

    def ffn_in_item(name, cols):
        def run():
            tmp[name] = _dot(tmp["hb"], win_ref[:, cols])
        return run

    def ffn_out_item(cols):
        def run():
            hidden = (_silu(tmp.pop("a")) * tmp.pop("b")).astype(BF16)
            part = _dot(hidden, wout_ref[cols, :])
            tmp["y"] = part if "y" not in tmp else tmp["y"] + part
        return run

    items = [mix_item]
    start = 0
    for width in FF_GROUPS:
        cols = slice(start, start + width)
        gate_cols = slice(D_FF + start, D_FF + start + width)
        items += [ffn_in_item("a", cols), ffn_in_item("b", gate_cols), ffn_out_item(cols)]
        start += width
    fill, drain = _make_fill(items, 3 * nch)

    tmp["y_gla"] = _dot(a_ref[...], wgo_ref[...])

    for ci in range(nch):
        rows = slice(ci * CHUNK, (ci + 1) * CHUNK)
        b_mid = fst_ref[0, 0, ci:ci + 1, 0:D_KEY]
        b_last = fst_ref[0, 0, ci:ci + 1, D_KEY:]
        o = _gla_chunk(qf_ref[0, rows, :], kf_ref[0, rows, :], v_ref[0, rows, :], st_ref,
                       jnp.exp(b_mid), jnp.exp(b_last), jnp.exp(b_last - b_mid), False, fill)
        o = o + ob_ref[0, rows, :]
        for h in range(N_HEADS):
            vs = slice(h * HEAD_V, (h + 1) * HEAD_V)
            a_ref[rows, vs] = (_rms(o[:, vs]) * gh_ref[...]
                               * gs_ref[0, rows, vs].astype(F32)).astype(BF16)
    drain()

    x2 = tmp["x1"] + mod_ref[0, 5:6, :] * tmp["y"]
    o_ref[0] = _rms(x2) * gfin_ref[...]


def _fwd_ffn_call(x, mod, qf, kf, v, ob, gs, sga, mb, fst, g_head, w_go, w_mo, s0, g_ffn, w_in,
                  w_out, g_final):
    bsz, t, _ = x.shape
    tm = TM_FWD
    nt = t // tm
    nch = tm // CHUNK
    steps = bsz * nt
    fst = fst.reshape(bsz, nt, nch, 2 * D_KEY)
    scan = lambda s: (jnp.minimum(s, steps - 1) // nt, jnp.minimum(s, steps - 1) % nt, 0)
    rest = lambda s: (jnp.maximum(s - 1, 0) // nt, jnp.maximum(s - 1, 0) % nt, 0)
    const = lambda s: (0, 0)
    scan_tok = lambda width: pl.BlockSpec((1, tm, width), scan)
    rest_tok = lambda width: pl.BlockSpec((1, tm, width), rest)
    return pl.pallas_call(
        functools.partial(_fwd_ffn_kernel, nt, steps),
        grid=(steps + 1,),
        in_specs=[
            scan_tok(D_KEY), scan_tok(D_KEY), scan_tok(D_VAL), scan_tok(D_VAL), scan_tok(D_VAL),
            pl.BlockSpec((1, 1, nch, 2 * D_KEY), lambda s: scan(s) + (0,)),
            pl.BlockSpec((1, HEAD_V, D_KEY), lambda s: (jnp.minimum(s, steps - 1) // nt, 0, 0)),
            pl.BlockSpec((1, HEAD_V), const),
            rest_tok(D_MODEL), rest_tok(D_MODEL), rest_tok(D_MODEL),
            pl.BlockSpec((1, 6, D_MODEL), lambda s: (jnp.maximum(s - 1, 0) // nt, 0, 0)),
            pl.BlockSpec(w_go.shape, const),
            pl.BlockSpec(w_mo.shape, const),
            pl.BlockSpec((1, D_MODEL), const),
            pl.BlockSpec(w_in.shape, const),
            pl.BlockSpec(w_out.shape, const),
            pl.BlockSpec((1, D_MODEL), const),
        ],
        out_specs=rest_tok(D_MODEL),
        out_shape=jax.ShapeDtypeStruct((bsz, t, D_MODEL), F32),
        scratch_shapes=[pltpu.VMEM((HEAD_V, D_KEY), F32), pltpu.VMEM((tm, D_VAL), BF16)],
        compiler_params=pltpu.CompilerParams(
            dimension_semantics=("arbitrary",), vmem_limit_bytes=VMEM_LIMIT),
        name="fwd_ffn",
    )(qf, kf, v, ob, gs, fst, s0, g_head, x, sga, mb, mod, w_go, w_mo, g_ffn, w_in, w_out, g_final)


def kernel(x, c, ctx, c_ctx, w_ada, b_ada, g_mix, w_in, w_dec_up, b_dec, g_head, w_gla_out,
           w_conv, w_conv_out, w_mix_out, g_ffn, w_ffn_in, w_ffn_out, g_final):
    assert w_ada.shape[0] == 1, "single-layer block"
    bsz = x.shape[0]
    assert x.shape[1] % TM_REV == 0 and x.shape[1] % TM_FWD == 0
    assert ctx.shape[1] % CHUNK == 0 and sum(FF_GROUPS) == D_FF

    assert w_in.shape[2] == N_IN
    w_pack = _pack_call(jnp.transpose(w_in[0]))
    w_dec = jnp.zeros((LANES, 2 * D_KEY), F32)
    w_dec = w_dec.at[:DECAY_RANK, :D_KEY].set(w_dec_up[0, 0])
    w_dec = w_dec.at[DECAY_RANK:2 * DECAY_RANK, D_KEY:].set(w_dec_up[0, 1]).astype(BF16)
    b_dec2 = b_dec[0].reshape(1, 2 * D_KEY)

    n_rows = -(-(bsz + 1) // 8) * 8
    c_rows = jnp.zeros((n_rows, D_MODEL), F32).at[:bsz].set(c).at[bsz].set(c_ctx)
    mod = _ada_call(c_rows, w_ada[0], b_ada).reshape(n_rows, 6, D_MODEL)

    g_mix2 = g_mix.reshape(1, D_MODEL)
    s_cf, s_cb = _ctx_call(ctx, mod, g_mix2, w_pack, w_dec, b_dec2, bsz)
    (qf, kf, v, ob, gs, sga, mb, fst), (w_go, w_mo, w_fi, w_fo) = _proj_bwd_call(
        x, mod, g_mix2, w_pack, w_dec, b_dec2, w_conv[0], w_conv_out[0].astype(BF16), s_cb,
        (w_gla_out[0], w_mix_out[0], w_ffn_in[0], w_ffn_out[0]))
    return _fwd_ffn_call(x, mod, qf, kf, v, ob, gs, sga, mb, fst, g_head.reshape(1, HEAD_V),
                         w_go, w_mo, s_cf, g_ffn.reshape(1, D_MODEL), w_fi, w_fo,
                         g_final.reshape(1, D_MODEL))
```

```python
import functools

import jax
import jax.numpy as jnp
from jax import lax
from jax.experimental import pallas as pl
from jax.experimental.pallas import tpu as pltpu

D_MODEL = 1024
N_HEADS = 4
D_KEY = 512
D_VAL = 1024
HEAD_K = D_KEY // N_HEADS
HEAD_V = D_VAL // N_HEADS
DECAY_RANK = 16
DECAY_TAU = 16.0
CHUNK = 64
GRID_W = 64
D_FF = 2816
EPS = 1e-6

LANES = 128
BF16_ROWS = 16
MXU_TILE = 256
TM_REV = 512
TM_FWD = 512
PROJ_PARTS = 4
COL_Q, COL_K, COL_V, COL_G = 0, D_KEY, 2 * D_KEY, 2 * D_KEY + D_VAL
COL_DD = COL_G + D_VAL
COL_CB = COL_DD + D_MODEL
COL_CC, COL_CH, COL_GA, COL_GB = (COL_CB + D_MODEL, COL_CB + 2 * D_MODEL, COL_CB + 3 * D_MODEL,
                                  COL_CB + 4 * D_MODEL)
N_PACK = COL_GB + D_MODEL
N_IN = N_PACK - (COL_CB - COL_DD) + 2 * DECAY_RANK
PACK_COLS = 1024
FF_GROUPS = (512, 512, 512, 512, 512, 256)
VMEM_LIMIT = 60 * 1024 * 1024

BF16 = jnp.bfloat16
F32 = jnp.float32

NT_DIMS = (((1,), (1,)), ((), ()))
TN_DIMS = (((0,), (0,)), ((), ()))


def _dot(a, b):
    return jnp.dot(a, b, preferred_element_type=F32)


def _sigmoid(x):
    return 0.5 * jnp.tanh(0.5 * x) + 0.5


def _silu(x):
    return x * _sigmoid(x)


def _log_sigmoid(x):
    return jnp.minimum(x, 0.0) - jnp.log(1.0 + jnp.exp(-jnp.abs(x)))


def _rms(x):
    return x * lax.rsqrt(jnp.mean(x * x, axis=-1, keepdims=True) + EPS)


def _norm_modulate(x, g, shift, scale):
    return (_rms(x) * (g * (1.0 + scale)) + shift).astype(BF16)


def _cumsum_matrix(n, upper, group):
    r = lax.broadcasted_iota(jnp.int32, (n, n), 0)
    c = lax.broadcasted_iota(jnp.int32, (n, n), 1)
    same = (r // group) == (c // group)
    tri = (c >= r) if upper else (c <= r)
    return jnp.where(same & tri, 1.0, 0.0).astype(BF16)


def _chunk_cumsum(la, upper, group=CHUNK):
    rows = min(la.shape[0], MXU_TILE)
    assert rows % group == 0
    mat = _cumsum_matrix(rows, upper, group)
    hi = la.astype(BF16)
    lo = (la - hi.astype(F32)).astype(BF16)
    parts = [_dot(mat, hi[r:r + rows]) + _dot(mat, lo[r:r + rows])
             for r in range(0, la.shape[0], rows)]
    return parts[0] if len(parts) == 1 else jnp.concatenate(parts, axis=0)


def _log_decay(hb, wdd_ref, wdec_ref, bdec_ref):
    dd = _dot(hb, wdd_ref[...]).astype(BF16)
    zd = _dot(dd, wdec_ref[...]) + bdec_ref[...]
    return _log_sigmoid(zd) * (1.0 / DECAY_TAU)


def _kv_all_heads(kc, vc):
    v_stack = jnp.concatenate([vc[:, h * HEAD_V:(h + 1) * HEAD_V] for h in range(N_HEADS)], axis=0)
    r = lax.broadcasted_iota(jnp.int32, (N_HEADS * CHUNK, D_KEY), 0) // CHUNK
    c = lax.broadcasted_iota(jnp.int32, (N_HEADS * CHUNK, D_KEY), 1) // HEAD_K
    k_blk = jnp.where(r == c, jnp.concatenate([kc] * N_HEADS, axis=0), jnp.zeros((), kc.dtype))
    return lax.dot_general(v_stack, k_blk, TN_DIMS, preferred_element_type=F32)


def _make_fill(items, n_gaps):
    queue = list(reversed(items))
    total = len(items)
    calls = [0]

    def fill():
        calls[0] += 1
        due = min(total, -(-calls[0] * total // n_gaps))
        while total - len(queue) < due:
            queue.pop()()

    def drain():
        while queue:
            queue.pop()()

    return fill, drain


def _no_fill():
    pass


def _gla_chunk(qc, kc, vc, st_ref, e_mid, e_last, e_gap, upper, fill=_no_fill):
    r = lax.broadcasted_iota(jnp.int32, (CHUNK, CHUNK), 0)
    c = lax.broadcasted_iota(jnp.int32, (CHUNK, CHUNK), 1)
    keep = (c >= r) if upper else (c <= r)
    heads = [(slice(h * HEAD_K, (h + 1) * HEAD_K), slice(h * HEAD_V, (h + 1) * HEAD_V))
             for h in range(N_HEADS)]
    scores = [lax.dot_general(qc[:, ks], kc[:, ks], NT_DIMS, preferred_element_type=F32)
              for ks, _ in heads]
    fill()
    st = st_ref[...]
    st_in = (st * e_mid).astype(BF16)
    outs = []
    for (ks, vs), s in zip(heads, scores):
        a = jnp.where(keep, s, 0.0).astype(BF16)
        outs.append(_dot(a, vc[:, vs])
                    + lax.dot_general(qc[:, ks], st_in[:, ks], NT_DIMS, preferred_element_type=F32))
    fill()
    st_ref[...] = st * e_last + _kv_all_heads(kc, vc) * e_gap
    fill()
    return jnp.concatenate(outs, axis=-1)


def _pack_kernel(a_ref, b_ref, o_ref):
    j = pl.program_id(0)
    dd_block = COL_DD // PACK_COLS
    first_shifted = COL_CB // PACK_COLS
    dd_rows = 2 * DECAY_RANK

    @pl.when(j < dd_block)
    def _():
        o_ref[...] = a_ref[...].T.astype(BF16)

    @pl.when(j == dd_block)
    def _():
        row = lax.broadcasted_iota(jnp.int32, a_ref.shape, 0)
        o_ref[...] = jnp.where(row < dd_rows, a_ref[...], 0.0).T.astype(BF16)

    @pl.when((j > dd_block) & (j < first_shifted))
    def _():
        o_ref[...] = jnp.zeros(o_ref.shape, BF16)

    @pl.when(j >= first_shifted)
    def _():
        src = jnp.concatenate([a_ref[dd_rows:, :], b_ref[...]], axis=0)
        o_ref[...] = src.T.astype(BF16)


def _pack_call(w_t):
    dd_block = COL_DD // PACK_COLS
    first_shifted = COL_CB // PACK_COLS
    dd_rows = 2 * DECAY_RANK
    a_idx = lambda j: jnp.where(j <= dd_block, j, jnp.maximum(j - (first_shifted - dd_block), dd_block))
    return pl.pallas_call(
        _pack_kernel,
        grid=(N_PACK // PACK_COLS,),
        in_specs=[
            pl.BlockSpec((PACK_COLS, D_MODEL), lambda j: (a_idx(j), 0)),
            pl.BlockSpec((dd_rows, D_MODEL), lambda j: ((a_idx(j) + 1) * (PACK_COLS // dd_rows), 0)),
        ],
        out_specs=pl.BlockSpec((D_MODEL, PACK_COLS), lambda j: (0, j)),
        out_shape=jax.ShapeDtypeStruct((D_MODEL, N_PACK), BF16),
        name="pack_w_in",
    )(w_t, w_t)


def _ada_kernel(c_ref, w_ref, b_ref, o_ref):
    s = _silu(c_ref[...]).astype(BF16)
    o_ref[...] = _dot(s, w_ref[...].astype(BF16)) + b_ref[...]


def _ada_call(c_rows, w_ada, b_ada):
    n_rows = c_rows.shape[0]
    n_out = w_ada.shape[1]
    bn = D_MODEL
    return pl.pallas_call(
        _ada_kernel,
        grid=(n_out // bn,),
        in_specs=[
            pl.BlockSpec((n_rows, D_MODEL), lambda j: (0, 0)),
            pl.BlockSpec((D_MODEL, bn), lambda j: (0, j)),
            pl.BlockSpec((1, bn), lambda j: (0, j)),
        ],
        out_specs=pl.BlockSpec((n_rows, bn), lambda j: (0, j)),
        out_shape=jax.ShapeDtypeStruct((n_rows, n_out), F32),
        name="ada",
    )(c_rows, w_ada, b_ada)


def _ctx_kernel(x_ref, mod_ref, g_ref, wk_ref, wv_ref, wdd_ref, wdec_ref, bdec_ref,
                sf_ref, sb_ref):
    bsz, n, _ = x_ref.shape
    x = x_ref[...].reshape(bsz * n, D_MODEL)
    hb = _norm_modulate(x, g_ref[...], mod_ref[0, 0:1, :], mod_ref[0, 1:2, :])
    la = _log_decay(hb, wdd_ref, wdec_ref, bdec_ref)
    b_f = _chunk_cumsum(la[:, :D_KEY], False, n)
    b_b = _chunk_cumsum(la[:, D_KEY:], True, n)
    k = _dot(hb, wk_ref[...])
    v = _dot(hb, wv_ref[...]).astype(BF16)
    for b in range(bsz):
        rows = slice(b * n, (b + 1) * n)
        k_f = (k[rows] * jnp.exp(b_f[(b + 1) * n - 1:(b + 1) * n] - b_f[rows])).astype(BF16)
        k_b = (k[rows] * jnp.exp(b_b[b * n:b * n + 1] - b_b[rows])).astype(BF16)
        for h in range(N_HEADS):
            ks = slice(h * HEAD_K, (h + 1) * HEAD_K)
            v_h = v[rows, h * HEAD_V:(h + 1) * HEAD_V]
            sf_ref[b, :, ks] = lax.dot_general(v_h, k_f[:, ks], TN_DIMS, preferred_element_type=F32)
            sb_ref[b, :, ks] = lax.dot_general(v_h, k_b[:, ks], TN_DIMS, preferred_element_type=F32)


def _ctx_call(ctx, mod, g_mix, w_pack, w_dec, b_dec, ctx_row):
    bsz, n, _ = ctx.shape
    assert n == MXU_TILE, "one cumsum matmul group per context sequence"
    const = lambda b: (0, 0)
    w_cols = lambda width, start: pl.BlockSpec((D_MODEL, width), lambda b: (0, start // width),
                                               pipeline_mode=pl.Buffered(1))
    st_spec = pl.BlockSpec((bsz, HEAD_V, D_KEY), lambda b: (0, 0, 0))
    st_shape = jax.ShapeDtypeStruct((bsz, HEAD_V, D_KEY), F32)
    return pl.pallas_call(
        _ctx_kernel,
        grid=(1,),
        in_specs=[
            pl.BlockSpec((bsz, n, D_MODEL), lambda b: (0, 0, 0)),
            pl.BlockSpec((1, 6, D_MODEL), lambda b: (ctx_row, 0, 0)),
            pl.BlockSpec((1, D_MODEL), const),
            w_cols(D_KEY, COL_K),
            w_cols(D_VAL, COL_V),
            w_cols(LANES, COL_DD),
            pl.BlockSpec(w_dec.shape, const),
            pl.BlockSpec(b_dec.shape, const),
        ],
        out_specs=[st_spec, st_spec],
        out_shape=[st_shape, st_shape],
        compiler_params=pltpu.CompilerParams(vmem_limit_bytes=VMEM_LIMIT),
        name="ctx_states",
    )(ctx, mod, g_mix, w_pack, w_pack, w_pack, w_dec, b_dec)


def _proj_bwd_kernel(nt, x0_ref, mod0_ref, xn_ref, modn_ref, g_ref, wqk_ref, wv_ref, wg_ref, wdd_ref,
                     wdec_ref, bdec_ref, wcb_ref, wcc_ref, wch_ref, wga_ref, wgb_ref, wconv_ref,
                     wco_ref, s0_ref, later_f32_refs,
                     qf_ref, kf_ref, v_ref, ob_ref, gs_ref, sga_ref, mb_ref, fst_ref, later_bf16_refs,
                     st_ref, p_ref, hb_ref):
    tm = xn_ref.shape[1]
    nch = tm // CHUNK
    mid = CHUNK // 2
    s = pl.program_id(0)
    cur = s % 2

    @pl.when(s == 0)
    def _():
        hb_ref[0] = _norm_modulate(x0_ref[0], g_ref[...], mod0_ref[0, 0:1, :], mod0_ref[0, 1:2, :])

    @pl.when(s % nt == 0)
    def _():
        st_ref[...] = s0_ref[0]

    hb = hb_ref.at[cur]
    width = D_MODEL // PROJ_PARTS
    col_groups = tuple(slice(j * width, (j + 1) * width) for j in range(PROJ_PARTS))
    pos = lax.broadcasted_iota(jnp.int32, (tm, 1), 0) % GRID_W
    tmp = {}

    def proj(w_ref, cols):
        return _dot(hb[...], w_ref[:, cols])

    def gate_item(cols):
        def run():
            g = proj(wg_ref, cols)
            gs_ref[0, :, cols] = _silu(g).astype(BF16)
        return run

    def conv_in_item(name, w_ref, cols):
        def run():
            tmp[name] = proj(w_ref, cols)
        return run

    def conv_item(cols):
        def run():
            cb = proj(wcb_ref, cols)
            u = tmp.pop("cc") * tmp.pop("ch")
            u_prev = jnp.where(pos == 0, 0.0, pltpu.roll(u, 1, 0))
            u_next = jnp.where(pos == GRID_W - 1, 0.0, pltpu.roll(u, tm - 1, 0))
            uc = (u_prev * wconv_ref[0:1, cols] + u * wconv_ref[1:2, cols]
                  + u_next * wconv_ref[2:3, cols])
            p_ref[:, cols] = (cb * uc).astype(BF16)
        return run

    def conv_out_item(cols):
        def run():
            tmp["yc"] = _dot(p_ref[...], wco_ref[:, cols])
        return run

    def merge_gate_item(cols):
        def run():
            gb = proj(wgb_ref, cols)
            mb_ref[0, :, cols] = (_sigmoid(gb) * tmp.pop("yc")).astype(BF16)
        return run

    def gla_gate_item(cols):
        def run():
            ga = proj(wga_ref, cols)
            sga_ref[0, :, cols] = _sigmoid(ga).astype(BF16)
        return run

    def next_input_item():
        hb_ref[1 - cur] = _norm_modulate(xn_ref[0], g_ref[...], modn_ref[0, 0:1, :],
                                         modn_ref[0, 1:2, :])
        for src, dst in zip(later_f32_refs, later_bf16_refs):
            dst[...] = src[...].astype(BF16)

    items = [gate_item(c) for c in col_groups]
    for c in col_groups:
        items += [conv_in_item("cc", wcc_ref, c), conv_in_item("ch", wch_ref, c), conv_item(c)]
    items += [next_input_item]
    for c in col_groups:
        items += [conv_out_item(c), merge_gate_item(c)]
    items += [gla_gate_item(c) for c in col_groups]
    fill, drain = _make_fill(items, len(items))

    la = _log_decay(hb[...], wdd_ref, wdec_ref, bdec_ref)
    fill()
    b_f = _chunk_cumsum(la[:, :D_KEY], False)
    b_b = _chunk_cumsum(la[:, D_KEY:], True)
    fill()

    q = proj(wqk_ref, slice(0, D_KEY)) * (HEAD_K ** -0.5)
    k = proj(wqk_ref, slice(D_KEY, 2 * D_KEY))
    vb = proj(wv_ref, slice(0, D_VAL)).astype(BF16)
    v_ref[0] = vb

    for ci in reversed(range(nch)):
        rows = slice(ci * CHUNK, (ci + 1) * CHUNK)
        bc = b_f[rows]
        b_mid, b_last = bc[mid - 1:mid], bc[CHUNK - 1:CHUNK]
        d = bc - b_mid
        qf_ref[0, rows, :] = (q[rows] * jnp.exp(d)).astype(BF16)
        kf_ref[0, rows, :] = (k[rows] * jnp.exp(-d)).astype(BF16)
        fst_ref[0, 0, ci:ci + 1, :] = jnp.concatenate([b_mid, b_last], axis=-1)
        bc = b_b[rows]
        b_mid, b_last = bc[mid:mid + 1], bc[0:1]
        d = bc - b_mid
        qc = (q[rows] * jnp.exp(d)).astype(BF16)
        kc = (k[rows] * jnp.exp(-d)).astype(BF16)
        ob_ref[0, rows, :] = _gla_chunk(qc, kc, vb[rows], st_ref, jnp.exp(b_mid),
                                        jnp.exp(b_last), jnp.exp(b_last - b_mid), True, fill)
    drain()


N_PROJ_INPUTS = 19


def _proj_bwd_call(x, mod, g_mix, w_pack, w_dec, b_dec, w_conv, w_co, s0, later_weights):
    bsz, t, _ = x.shape
    tm = TM_REV
    nt = t // tm
    nch = tm // CHUNK
    steps = bsz * nt
    slab_specs = []
    for w in later_weights:
        rows = -(-w.shape[0] // (steps * BF16_ROWS)) * BF16_ROWS
        last = -(-w.shape[0] // rows) - 1
        slab_specs.append(pl.BlockSpec((rows, w.shape[1]), lambda s, last=last: (jnp.minimum(s, last), 0)))
    n_later = len(later_weights)
    cur = lambda s: (s // nt, nt - 1 - s % nt, 0)
    nxt = lambda s: cur(jnp.minimum(s + 1, steps - 1))
    const = lambda s: (0, 0)
    w_cols = lambda width, start: pl.BlockSpec((D_MODEL, width), lambda s: (0, start // width),
                                               pipeline_mode=pl.Buffered(1))
    tok = lambda width: pl.BlockSpec((1, tm, width), cur)
    tok_shape = lambda width, dt: jax.ShapeDtypeStruct((bsz, t, width), dt)
    def body(*refs):
        n_in = N_PROJ_INPUTS
        ins, rest = refs[:n_in], refs[n_in:]
        later_in, rest = rest[:n_later], rest[n_later:]
        outs, rest = rest[:8], rest[8:]
        later_out, scratch = rest[:n_later], rest[n_later:]
        _proj_bwd_kernel(nt, *ins, later_in, *outs, later_out, *scratch)

    res = pl.pallas_call(
        body,
        grid=(steps,),
        in_specs=[
            pl.BlockSpec((1, tm, D_MODEL), lambda s: (0, nt - 1, 0)),
            pl.BlockSpec((1, 6, D_MODEL), lambda s: (0, 0, 0)),
            pl.BlockSpec((1, tm, D_MODEL), nxt),
            pl.BlockSpec((1, 6, D_MODEL), lambda s: (jnp.minimum(s + 1, steps - 1) // nt, 0, 0)),
            pl.BlockSpec((1, D_MODEL), const),
            w_cols(2 * D_KEY, COL_Q), w_cols(D_VAL, COL_V), w_cols(D_VAL, COL_G), w_cols(LANES, COL_DD),
            pl.BlockSpec(w_dec.shape, const),
            pl.BlockSpec(b_dec.shape, const),
            w_cols(D_MODEL, COL_CB), w_cols(D_MODEL, COL_CC), w_cols(D_MODEL, COL_CH),
            w_cols(D_MODEL, COL_GA), w_cols(D_MODEL, COL_GB),
            pl.BlockSpec(w_conv.shape, const),
            pl.BlockSpec(w_co.shape, const),
            pl.BlockSpec((1, HEAD_V, D_KEY), lambda s: (s // nt, 0, 0)),
        ] + slab_specs,
        out_specs=[
            tok(D_KEY), tok(D_KEY), tok(D_VAL), tok(D_VAL), tok(D_VAL), tok(D_MODEL), tok(D_MODEL),
            pl.BlockSpec((1, 1, nch, 2 * D_KEY), lambda s: (s // nt, nt - 1 - s % nt, 0, 0)),
        ] + slab_specs,
        out_shape=[
            tok_shape(D_KEY, BF16), tok_shape(D_KEY, BF16), tok_shape(D_VAL, BF16),
            tok_shape(D_VAL, F32), tok_shape(D_VAL, BF16), tok_shape(D_MODEL, BF16),
            tok_shape(D_MODEL, BF16),
            jax.ShapeDtypeStruct((bsz, nt, nch, 2 * D_KEY), F32),
        ] + [jax.ShapeDtypeStruct(w.shape, BF16) for w in later_weights],
        scratch_shapes=[pltpu.VMEM((HEAD_V, D_KEY), F32), pltpu.VMEM((tm, D_MODEL), BF16),
                        pltpu.VMEM((2, tm, D_MODEL), BF16)],
        compiler_params=pltpu.CompilerParams(
            dimension_semantics=("arbitrary",), vmem_limit_bytes=VMEM_LIMIT),
        name="proj_bwd",
    )(x, mod, x, mod, g_mix, w_pack, w_pack, w_pack, w_pack, w_dec, b_dec,
      w_pack, w_pack, w_pack, w_pack, w_pack, w_conv, w_co, s0, *later_weights)
    return res[:8], res[8:]


def _fwd_ffn_kernel(nt, steps, qf_ref, kf_ref, v_ref, ob_ref, gs_ref, fst_ref, s0_ref, gh_ref,
                    x_ref, sga_ref, mb_ref, mod_ref, wgo_ref, wmo_ref, gffn_ref, win_ref, wout_ref,
                    gfin_ref, o_ref, st_ref, a_ref):
    tm = x_ref.shape[1]
    nch = tm // CHUNK
    s = pl.program_id(0)
    t = jnp.minimum(s, steps - 1)

    @pl.when(s == 0)
    def _():
        a_ref[...] = jnp.zeros(a_ref.shape, a_ref.dtype)

    @pl.when(t % nt == 0)
    def _():
        st_ref[...] = s0_ref[0]

    tmp = {}

    def mix_item():
        merged = sga_ref[0].astype(F32) * tmp.pop("y_gla") + mb_ref[0].astype(F32)
        mix = _dot(merged.astype(BF16), wmo_ref[...])
        x1 = x_ref[0] + mod_ref[0, 2:3, :] * mix
        tmp["x1"] = x1
        tmp["hb"] = _norm_modulate(x1, gffn_ref[...], mod_ref[0, 3:4, :], mod_ref[0, 4:5, :])

    def ffn_in_item(name, cols):
        def run():
            tmp[name] = _dot(tmp["hb"], win_ref[:, cols])
        return run

    def ffn_out_item(cols):
        def run():
            hidden = (_silu(tmp.pop("a")) * tmp.pop("b")).astype(BF16)
            part = _dot(hidden, wout_ref[cols, :])
            tmp["y"] = part if "y" not in tmp else tmp["y"] + part
        return run

    items = [mix_item]
    start = 0
    for width in FF_GROUPS:
        cols = slice(start, start + width)
        gate_cols = slice(D_FF + start, D_FF + start + width)
        items += [ffn_in_item("a", cols), ffn_in_item("b", gate_cols), ffn_out_item(cols)]
        start += width
    fill, drain = _make_fill(items, 3 * nch)

    tmp["y_gla"] = _dot(a_ref[...], wgo_ref[...])

    for ci in range(nch):
        rows = slice(ci * CHUNK, (ci + 1) * CHUNK)
        b_mid = fst_ref[0, 0, ci:ci + 1, 0:D_KEY]
        b_last = fst_ref[0, 0, ci:ci + 1, D_KEY:]
        o = _gla_chunk(qf_ref[0, rows, :], kf_ref[0, rows, :], v_ref[0, rows, :], st_ref,
                       jnp.exp(b_mid), jnp.exp(b_last), jnp.exp(b_last - b_mid), False, fill)
        o = o + ob_ref[0, rows, :]
        for h in range(N_HEADS):
            vs = slice(h * HEAD_V, (h + 1) * HEAD_V)
            a_ref[rows, vs] = (_rms(o[:, vs]) * gh_ref[...]
                               * gs_ref[0, rows, vs].astype(F32)).astype(BF16)
    drain()

    x2 = tmp["x1"] + mod_ref[0, 5:6, :] * tmp["y"]
    o_ref[0] = _rms(x2) * gfin_ref[...]


def _fwd_ffn_call(x, mod, qf, kf, v, ob, gs, sga, mb, fst, g_head, w_go, w_mo, s0, g_ffn, w_in,
                  w_out, g_final):
    bsz, t, _ = x.shape
    tm = TM_FWD
    nt = t // tm
    nch = tm // CHUNK
    steps = bsz * nt
    fst = fst.reshape(bsz, nt, nch, 2 * D_KEY)
    scan = lambda s: (jnp.minimum(s, steps - 1) // nt, jnp.minimum(s, steps - 1) % nt, 0)
    rest = lambda s: (jnp.maximum(s - 1, 0) // nt, jnp.maximum(s - 1, 0) % nt, 0)
    const = lambda s: (0, 0)
    scan_tok = lambda width: pl.BlockSpec((1, tm, width), scan)
    rest_tok = lambda width: pl.BlockSpec((1, tm, width), rest)
    return pl.pallas_call(
        functools.partial(_fwd_ffn_kernel, nt, steps),
        grid=(steps + 1,),
        in_specs=[
            scan_tok(D_KEY), scan_tok(D_KEY), scan_tok(D_VAL), scan_tok(D_VAL), scan_tok(D_VAL),
            pl.BlockSpec((1, 1, nch, 2 * D_KEY), lambda s: scan(s) + (0,)),
            pl.BlockSpec((1, HEAD_V, D_KEY), lambda s: (jnp.minimum(s, steps - 1) // nt, 0, 0)),
            pl.BlockSpec((1, HEAD_V), const),
            rest_tok(D_MODEL), rest_tok(D_MODEL), rest_tok(D_MODEL),
            pl.BlockSpec((1, 6, D_MODEL), lambda s: (jnp.maximum(s - 1, 0) // nt, 0, 0)),
            pl.BlockSpec(w_go.shape, const),
            pl.BlockSpec(w_mo.shape, const),
            pl.BlockSpec((1, D_MODEL), const),
            pl.BlockSpec(w_in.shape, const),
            pl.BlockSpec(w_out.shape, const),
            pl.BlockSpec((1, D_MODEL), const),
        ],
        out_specs=rest_tok(D_MODEL),
        out_shape=jax.ShapeDtypeStruct((bsz, t, D_MODEL), F32),
        scratch_shapes=[pltpu.VMEM((HEAD_V, D_KEY), F32), pltpu.VMEM((tm, D_VAL), BF16)],
        compiler_params=pltpu.CompilerParams(
            dimension_semantics=("arbitrary",), vmem_limit_bytes=VMEM_LIMIT),
        name="fwd_ffn",
    )(qf, kf, v, ob, gs, fst, s0, g_head, x, sga, mb, mod, w_go, w_mo, g_ffn, w_in, w_out, g_final)


def kernel(x, c, ctx, c_ctx, w_ada, b_ada, g_mix, w_in, w_dec_up, b_dec, g_head, w_gla_out,
           w_conv, w_conv_out, w_mix_out, g_ffn, w_ffn_in, w_ffn_out, g_final):
    assert w_ada.shape[0] == 1, "single-layer block"
    bsz = x.shape[0]
    assert x.shape[1] % TM_REV == 0 and x.shape[1] % TM_FWD == 0
    assert ctx.shape[1] % CHUNK == 0 and sum(FF_GROUPS) == D_FF

    assert w_in.shape[2] == N_IN
    w_pack = _pack_call(jnp.transpose(w_in[0]))
    w_dec = jnp.zeros((LANES, 2 * D_KEY), F32)
    w_dec = w_dec.at[:DECAY_RANK, :D_KEY].set(w_dec_up[0, 0])
    w_dec = w_dec.at[DECAY_RANK:2 * DECAY_RANK, D_KEY:].set(w_dec_up[0, 1]).astype(BF16)
    b_dec2 = b_dec[0].reshape(1, 2 * D_KEY)

    n_rows = -(-(bsz + 1) // 8) * 8
    c_rows = jnp.zeros((n_rows, D_MODEL), F32).at[:bsz].set(c).at[bsz].set(c_ctx)
    mod = _ada_call(c_rows, w_ada[0], b_ada).reshape(n_rows, 6, D_MODEL)

    g_mix2 = g_mix.reshape(1, D_MODEL)
    s_cf, s_cb = _ctx_call(ctx, mod, g_mix2, w_pack, w_dec, b_dec2, bsz)
    (qf, kf, v, ob, gs, sga, mb, fst), (w_go, w_mo, w_fi, w_fo) = _proj_bwd_call(
        x, mod, g_mix2, w_pack, w_dec, b_dec2, w_conv[0], w_conv_out[0].astype(BF16), s_cb,
        (w_gla_out[0], w_mix_out[0], w_ffn_in[0], w_ffn_out[0]))
    return _fwd_ffn_call(x, mod, qf, kf, v, ob, gs, sga, mb, fst, g_head.reshape(1, HEAD_V),
                         w_go, w_mo, s_cf, g_ffn.reshape(1, D_MODEL), w_fi, w_fo,
                         g_final.reshape(1, D_MODEL))
```
